```python
import math
import jax, jax.numpy as jnp
from jax import lax
import numpy as np

D_MODEL = 1024
BATCH = 4
SEQ = 8192
DEPTH = 2
DEC_BATCH = 4
DEC_SEQ = 4096
PAST_LEN = 128

D_INNER = 2 * D_MODEL
SSM_HEADDIM = 64
SSM_HEADS = D_INNER // SSM_HEADDIM
SSM_GROUPS = 4
SSM_STATE = 128
SSM_CONV = 5
SSM_CHUNK = 128
SSM_CONV_DIM = D_INNER + 2 * SSM_GROUPS * SSM_STATE
D_CONV = D_MODEL
CONF_KERNEL = 31
FFN_DIM = 2816
HALF_STEP = 0.5
N_MOD = 9
EPS = 1e-6
OFF_XBC = D_INNER
OFF_DT = OFF_XBC + SSM_CONV_DIM
OFF_GLU = OFF_DT + 2 * SSM_HEADS
OFF_GATE = OFF_GLU + 2 * D_CONV
IN_COLS = OFF_GATE + 2 * D_MODEL

kernel_name = 'hybrid_ssd_conformer_encoder'


def rms_norm(x, g):
    xf = x.astype(jnp.float32)
    y = xf * lax.rsqrt(jnp.mean(xf * xf, axis=-1, keepdims=True) + EPS)
    return (y * g.astype(jnp.float32)).astype(x.dtype)


def layer_norm(x, g, b):
    xf = x.astype(jnp.float32)
    mu = jnp.mean(xf, axis=-1, keepdims=True)
    var = jnp.mean(jnp.square(xf - mu), axis=-1, keepdims=True)
    y = (xf - mu) * lax.rsqrt(var + EPS)
    return (y * g.astype(jnp.float32) + b.astype(jnp.float32)).astype(x.dtype)


def modulate(h, shift, scale):
    return h * (1 + scale[:, None, :]) + shift[:, None, :]


def swiglu(h, w_up, w_down):
    a, b = jnp.split(h @ w_up, 2, axis=-1)
    return (jax.nn.silu(a) * b) @ w_down


def depthwise_conv(x, w, b):
    k, ch = w.shape
    y = lax.conv_general_dilated(x, w[:, None, :], window_strides=(1,),
                                 padding=[(k // 2, k // 2)],
                                 dimension_numbers=('NWC', 'WIO', 'NWC'),
                                 feature_group_count=ch)
    return y + b


def segsum(a):
    t = a.shape[-1]
    a_rep = jnp.broadcast_to(a[..., None], a.shape + (t,))
    a_rep = jnp.where(jnp.tril(jnp.ones((t, t), bool), -1), a_rep, 0.0)
    ss = jnp.cumsum(a_rep, axis=-2)
    return jnp.where(jnp.tril(jnp.ones((t, t), bool)), ss, -jnp.inf)


def ssd_chunked(x, dt, a, bm, cm):
    bsz, t, h, p = x.shape
    g, n = bm.shape[2], bm.shape[3]
    k = h // g
    c = t // SSM_CHUNK
    L = SSM_CHUNK
    xs = (x * dt[..., None]).reshape(bsz, c, L, g, k, p)
    da = (dt * a).reshape(bsz, c, L, g, k).transpose(0, 3, 4, 1, 2)
    bc = bm.reshape(bsz, c, L, g, n)
    cc = cm.reshape(bsz, c, L, g, n)
    a_cs = jnp.cumsum(da, axis=-1)
    lmat = jnp.exp(segsum(da))
    cb = jnp.einsum('bclgn,bcsgn->bgcls', cc, bc)
    y_diag = jnp.einsum('bgcls,bgkcls,bcsgkp->bclgkp', cb, lmat, xs)
    decay_states = jnp.exp(a_cs[..., -1:] - a_cs)
    states = jnp.einsum('bclgn,bgkcl,bclgkp->bcgkpn', bc, decay_states, xs)
    a_last = jnp.pad(a_cs[..., -1], [(0, 0), (0, 0), (0, 0), (1, 0)])
    decay_chunk = jnp.exp(segsum(a_last))
    states = jnp.concatenate([jnp.zeros_like(states[:, :1]), states], axis=1)
    states = jnp.einsum('bgkzc,bcgkpn->bzgkpn', decay_chunk, states)[:, :-1]
    y_off = jnp.einsum('bclgn,bcgkpn,bgkcl->bclgkp', cc, states, jnp.exp(a_cs))
    return (y_diag + y_off).reshape(bsz, t, h, p)


def ssd_bidirectional(xs, dt_raw, bm, cm, dt_bias, a_log, d_skip):
    f32 = jnp.float32
    xf, bf, cf = xs.astype(f32), bm.astype(f32), cm.astype(f32)
    dtf = dt_raw.astype(f32)
    dt_fw = jax.nn.softplus(dtf[..., :SSM_HEADS] + dt_bias[0].astype(f32))
    dt_bw = jax.nn.softplus(dtf[..., SSM_HEADS:] + dt_bias[1].astype(f32))
    a_fw = -jnp.exp(a_log[0].astype(f32))
    a_bw = -jnp.exp(a_log[1].astype(f32))
    flip = lambda v: jnp.flip(v, axis=1)
    y_fw = ssd_chunked(xf, dt_fw, a_fw, bf, cf)
    y_bw = flip(ssd_chunked(flip(xf), flip(dt_bw), a_bw, flip(bf), flip(cf)))
    return y_fw + y_bw + d_skip.astype(f32)[:, None] * xf


def hybrid_mixer(u, w_in, ssm_conv_w, ssm_conv_b, dt_bias, a_log, d_skip, ssm_norm,
                 w_proj_ssd, conf_conv_w, conf_conv_b, conf_ln_g, conf_ln_b, w_proj_conv, w_out):
    bsz, t, _ = u.shape
    proj = u @ w_in
    z, xbc, dt_raw, glu_in, gate_in = jnp.split(proj, [OFF_XBC, OFF_DT, OFF_GLU, OFF_GATE], axis=-1)
    xbc = jax.nn.silu(depthwise_conv(xbc, ssm_conv_w, ssm_conv_b))
    xs, bm, cm = jnp.split(xbc, [D_INNER, D_INNER + SSM_GROUPS * SSM_STATE], axis=-1)
    xs = xs.reshape(bsz, t, SSM_HEADS, SSM_HEADDIM)
    bm = bm.reshape(bsz, t, SSM_GROUPS, SSM_STATE)
    cm = cm.reshape(bsz, t, SSM_GROUPS, SSM_STATE)
    y = ssd_bidirectional(xs, dt_raw, bm, cm, dt_bias, a_log, d_skip).reshape(bsz, t, D_INNER)
    y = rms_norm(y * jax.nn.silu(z.astype(jnp.float32)), ssm_norm).astype(u.dtype)
    br_a = y @ w_proj_ssd
    va, vg = jnp.split(glu_in, 2, axis=-1)
    v = va * jax.nn.sigmoid(vg)
    v = depthwise_conv(v, conf_conv_w, conf_conv_b)
    v = jax.nn.silu(layer_norm(v, conf_ln_g, conf_ln_b))
    br_b = v @ w_proj_conv
    g_a, g_b = jnp.split(jax.nn.sigmoid(gate_in), 2, axis=-1)
    return (g_a * br_a + g_b * br_b) @ w_out


def encoder_trunk(x, c, w_ada, b_ada, ffn1_norm, ffn1_up, ffn1_down, mix_norm, w_in,
                  ssm_conv_w, ssm_conv_b, dt_bias, a_log, d_skip, ssm_norm, w_proj_ssd,
                  conf_conv_w, conf_conv_b, conf_ln_g, conf_ln_b, w_proj_conv, w_out,
                  ffn2_norm, ffn2_up, ffn2_down, final_norm):
    h = x
    c_act = jax.nn.silu(c)
    for l in range(DEPTH):
        mods = c_act @ w_ada[l] + b_ada[l]
        sh1, sc1, g1, sh2, sc2, g2, sh3, sc3, g3 = jnp.split(mods, N_MOD, axis=-1)
        u = modulate(rms_norm(h, ffn1_norm[l]), sh1, sc1)
        h = h + HALF_STEP * g1[:, None, :] * swiglu(u, ffn1_up[l], ffn1_down[l])
        u = modulate(rms_norm(h, mix_norm[l]), sh2, sc2)
        h = h + g2[:, None, :] * hybrid_mixer(u, w_in[l], ssm_conv_w[l], ssm_conv_b[l], dt_bias[l],
                                             a_log[l], d_skip[l], ssm_norm[l], w_proj_ssd[l],
                                             conf_conv_w[l], conf_conv_b[l], conf_ln_g[l],
                                             conf_ln_b[l], w_proj_conv[l], w_out[l])
        u = modulate(rms_norm(h, ffn2_norm[l]), sh3, sc3)
        h = h + HALF_STEP * g3[:, None, :] * swiglu(u, ffn2_up[l], ffn2_down[l])
    return rms_norm(h, final_norm)


def setup_inputs(seed: int = 0) -> dict:
    key = jax.random.key(seed)
    ks = iter(list(jax.random.split(key, 48)))
    f32 = jnp.float32
    L, D = DEPTH, D_MODEL

    def nrm(shape, scale):
        return scale * jax.random.normal(next(ks), shape, f32)

    dt0 = jnp.exp(jax.random.uniform(next(ks), (L, 2, SSM_HEADS), f32, math.log(1e-3), math.log(1e-1)))
    dt_bias = dt0 + jnp.log(-jnp.expm1(-dt0))
    a_log = jnp.log(jax.random.uniform(next(ks), (L, 2, SSM_HEADS), f32, 1.0, 16.0))
    return {
        'x_prompt': nrm((BATCH, SEQ, D), 1.0),
        'x_sample': nrm((DEC_BATCH, DEC_SEQ, D), 1.0),
        'c_prompt': nrm((BATCH, D), 1.0),
        'c_sample': nrm((DEC_BATCH, D), 1.0),
        'w_ada': nrm((L, D, N_MOD * D), 0.5 * D ** -0.5),
        'b_ada': nrm((L, N_MOD * D), 0.02),
        'ffn1_norm': 1.0 + nrm((L, D), 0.02),
        'ffn1_up': nrm((L, D, 2 * FFN_DIM), D ** -0.5),
        'ffn1_down': nrm((L, FFN_DIM, D), FFN_DIM ** -0.5),
        'mix_norm': 1.0 + nrm((L, D), 0.02),
        'w_in': nrm((L, D, IN_COLS), D ** -0.5),
        'ssm_conv_w': nrm((L, SSM_CONV, SSM_CONV_DIM), SSM_CONV ** -0.5),
        'ssm_conv_b': nrm((L, SSM_CONV_DIM), 0.02),
        'dt_bias': dt_bias,
        'a_log': a_log,
        'd_skip': 1.0 + nrm((L, SSM_HEADS), 0.1),
        'ssm_norm': 1.0 + nrm((L, D_INNER), 0.02),
        'w_proj_ssd': nrm((L, D_INNER, D), D_INNER ** -0.5),
        'conf_conv_w': nrm((L, CONF_KERNEL, D_CONV), CONF_KERNEL ** -0.5),
        'conf_conv_b': nrm((L, D_CONV), 0.02),
        'conf_ln_g': 1.0 + nrm((L, D_CONV), 0.02),
        'conf_ln_b': nrm((L, D_CONV), 0.02),
        'w_proj_conv': nrm((L, D_CONV, D), D_CONV ** -0.5),
        'w_out': nrm((L, D, D), D ** -0.5),
        'ffn2_norm': 1.0 + nrm((L, D), 0.02),
        'ffn2_up': nrm((L, D, 2 * FFN_DIM), D ** -0.5),
        'ffn2_down': nrm((L, FFN_DIM, D), FFN_DIM ** -0.5),
        'final_norm': 1.0 + nrm((D,), 0.02),
    }


def reference(x_prompt, x_sample, c_prompt, c_sample, w_ada, b_ada, ffn1_norm, ffn1_up, ffn1_down,
              mix_norm, w_in, ssm_conv_w, ssm_conv_b, dt_bias, a_log, d_skip, ssm_norm, w_proj_ssd,
              conf_conv_w, conf_conv_b, conf_ln_g, conf_ln_b, w_proj_conv, w_out,
              ffn2_norm, ffn2_up, ffn2_down, final_norm):
    weights = (w_ada, b_ada, ffn1_norm, ffn1_up, ffn1_down, mix_norm, w_in, ssm_conv_w, ssm_conv_b,
               dt_bias, a_log, d_skip, ssm_norm, w_proj_ssd, conf_conv_w, conf_conv_b, conf_ln_g,
               conf_ln_b, w_proj_conv, w_out, ffn2_norm, ffn2_up, ffn2_down, final_norm)
    y_prompt = encoder_trunk(x_prompt, c_prompt, *weights)
    y_sample = encoder_trunk(x_sample, c_sample, *weights)
    return (y_prompt, y_sample)
```

```python
import functools

import jax
import jax.numpy as jnp
from jax import lax
from jax.experimental import pallas as pl
from jax.experimental.pallas import tpu as pltpu

D_MODEL = 1024
D_INNER = 2 * D_MODEL
HEADDIM = 64
HEADS = D_INNER // HEADDIM
GROUPS = 4
HEADS_PER_GROUP = HEADS // GROUPS
NSTATE = 128
CHUNK = 128
SSM_K = 5
CONV_DIM = D_INNER + 2 * GROUPS * NSTATE
CONF_K = 31
FFN_DIM = 2816
N_MOD = 9
EPS = 1e-6
OFF_XBC = D_INNER
OFF_DT = OFF_XBC + CONV_DIM
OFF_GLU = OFF_DT + 2 * HEADS
OFF_GATE = OFF_GLU + 2 * D_MODEL
IN_COLS = OFF_GATE + 2 * D_MODEL

LANES = 128
HALO = 16
FFN_COLS = 256
PROJ_COLS = 512
VMEM_LIMIT = 56 * 1024 * 1024

F32 = jnp.float32
BF16 = jnp.bfloat16


def _dot(a, b):
    return jnp.dot(a, b, preferred_element_type=F32)


def _rms_mod(x, norm_w, shift, scale):
    y = x * lax.rsqrt(jnp.mean(x * x, axis=-1, keepdims=True) + EPS)
    return (y * norm_w) * (1.0 + scale) + shift


def _silu(x):
    return x * jax.nn.sigmoid(x)


def _const_spec(shape):
    nd = len(shape)
    return pl.BlockSpec(shape, lambda *_: (0,) * nd, pipeline_mode=pl.Buffered(1))


def _mods_kernel(c_ref, w_ref, b_ref, o_ref):
    c_act = _silu(c_ref[...])
    o_ref[0] = jnp.dot(c_act, w_ref[0], preferred_element_type=F32,
                       precision=lax.Precision.HIGHEST) + b_ref[0]


def _mods(c_all, w_ada, b_ada):
    depth, _, cols = w_ada.shape
    rows = c_all.shape[0]
    tn = D_MODEL
    return pl.pallas_call(
        _mods_kernel,
        grid=(depth, cols // tn),
        in_specs=[
            pl.BlockSpec((rows, D_MODEL), lambda l, j: (0, 0)),
            pl.BlockSpec((1, D_MODEL, tn), lambda l, j: (l, 0, j)),
            pl.BlockSpec((1, 1, tn), lambda l, j: (l, 0, j)),
        ],
        out_specs=pl.BlockSpec((1, rows, tn), lambda l, j: (l, 0, j)),
        out_shape=jax.ShapeDtypeStruct((depth, rows, cols), F32),
        name="adaln_mods",
    )(c_all, w_ada, b_ada.reshape(depth, 1, cols))


def _ffn_kernel(h_ref, mod_ref, nw_ref, wup_ref, wdn_ref, fnw_ref, o_ref, act_ref, *, mod_base, final):
    x = h_ref[0]
    shift = mod_ref[0, mod_base:mod_base + 1, :]
    scale = mod_ref[0, mod_base + 1:mod_base + 2, :]
    gate = mod_ref[0, mod_base + 2:mod_base + 3, :]
    u = _rms_mod(x, nw_ref[...], shift, scale).astype(BF16)
    for j in range(FFN_DIM // FFN_COLS):
        cols = slice(j * FFN_COLS, (j + 1) * FFN_COLS)
        gcols = slice(FFN_DIM + j * FFN_COLS, FFN_DIM + (j + 1) * FFN_COLS)
        a = _dot(u, wup_ref[:, cols])
        b = _dot(u, wup_ref[:, gcols])
        act_ref[:, cols] = (_silu(a) * b).astype(BF16)
    y = _dot(act_ref[...], wdn_ref[...])
    out = x + (0.5 * gate) * y
    if final:
        out = out * lax.rsqrt(jnp.mean(out * out, axis=-1, keepdims=True) + EPS) * fnw_ref[...]
    o_ref[0] = out


def _ffn(h, mods, norm_w, w_up, w_down, final_w, *, mod_base, final, tm=512):
    bsz, seq, _ = h.shape
    kern = functools.partial(_ffn_kernel, mod_base=mod_base, final=final)
    return pl.pallas_call(
        kern,
        grid=(bsz, seq // tm),
        in_specs=[
            pl.BlockSpec((1, tm, D_MODEL), lambda b, i: (b, i, 0)),
            pl.BlockSpec((1, N_MOD, D_MODEL), lambda b, i: (b, 0, 0)),
            _const_spec((1, D_MODEL)),
            _const_spec((D_MODEL, 2 * FFN_DIM)),
            _const_spec((FFN_DIM, D_MODEL)),
            _const_spec((1, D_MODEL)),
        ],
        out_specs=pl.BlockSpec((1, tm, D_MODEL), lambda b, i: (b, i, 0)),
        out_shape=jax.ShapeDtypeStruct(h.shape, F32),
        scratch_shapes=[pltpu.VMEM((tm, FFN_DIM), BF16)],
        compiler_params=pltpu.CompilerParams(
            dimension_semantics=("parallel", "parallel"), vmem_limit_bytes=VMEM_LIMIT),
        name="ffn_final" if final else "ffn",
    )(h, mods, norm_w.reshape(1, D_MODEL), w_up.astype(BF16), w_down.astype(BF16),
      final_w.reshape(1, D_MODEL))


PW_Z = 0
PW_XBC = PW_Z + D_INNER
PW_GLU = PW_XBC + CONV_DIM
PW_GATE = PW_GLU + 2 * D_MODEL
PW_DT = PW_GATE + 2 * D_MODEL
PW_COLS = PW_DT + LANES


def _inproj_kernel(h_ref, mod_ref, nw_ref, w_ref, dtb_ref, zs_ref, xbc_ref, v_ref, gates_ref, dt_ref):
    x = h_ref[0]
    u = _rms_mod(x, nw_ref[...], mod_ref[0, 3:4, :], mod_ref[0, 4:5, :]).astype(BF16)
    for j in range(D_INNER // PROJ_COLS):
        cols = slice(j * PROJ_COLS, (j + 1) * PROJ_COLS)
        z = _dot(u, w_ref[:, PW_Z + j * PROJ_COLS:PW_Z + (j + 1) * PROJ_COLS])
        zs_ref[0, :, cols] = _silu(z).astype(BF16)
    for j in range(CONV_DIM // PROJ_COLS):
        cols = slice(j * PROJ_COLS, (j + 1) * PROJ_COLS)
        xbc_ref[0, :, cols] = _dot(u, w_ref[:, PW_XBC + j * PROJ_COLS:PW_XBC + (j + 1) * PROJ_COLS]).astype(BF16)
    for j in range(D_MODEL // PROJ_COLS):
        cols = slice(j * PROJ_COLS, (j + 1) * PROJ_COLS)
        va = _dot(u, w_ref[:, PW_GLU + j * PROJ_COLS:PW_GLU + (j + 1) * PROJ_COLS])
        vg = _dot(u, w_ref[:, PW_GLU + D_MODEL + j * PROJ_COLS:PW_GLU + D_MODEL + (j + 1) * PROJ_COLS])
        v_ref[0, :, cols] = (va * jax.nn.sigmoid(vg)).astype(BF16)
    for j in range(2 * D_MODEL // PROJ_COLS):
        cols = slice(j * PROJ_COLS, (j + 1) * PROJ_COLS)
        gt = _dot(u, w_ref[:, PW_GATE + j * PROJ_COLS:PW_GATE + (j + 1) * PROJ_COLS])
        gates_ref[0, :, cols] = jax.nn.sigmoid(gt).astype(BF16)
    dtr = _dot(u, w_ref[:, PW_DT:PW_DT + LANES]) + dtb_ref[...]
    dt_ref[0] = jnp.maximum(dtr, 0.0) + jnp.log(1.0 + jnp.exp(-jnp.abs(dtr)))


def _inproj(h, mods, norm_w, w_in, dt_bias, *, tm=512):
    bsz, seq, _ = h.shape
    w = jnp.concatenate([
        w_in[:, :OFF_DT], w_in[:, OFF_GLU:], w_in[:, OFF_DT:OFF_GLU],
        jnp.zeros((D_MODEL, LANES - 2 * HEADS), w_in.dtype)], axis=1).astype(BF16)
    dtb = jnp.concatenate([dt_bias.reshape(1, 2 * HEADS), jnp.zeros((1, LANES - 2 * HEADS), F32)], axis=1)

    def tok(width):
        return pl.BlockSpec((1, tm, width), lambda b, i: (b, i, 0))

    return pl.pallas_call(
        _inproj_kernel,
        grid=(bsz, seq // tm),
        in_specs=[
            tok(D_MODEL),
            pl.BlockSpec((1, N_MOD, D_MODEL), lambda b, i: (b, 0, 0)),
            _const_spec((1, D_MODEL)),
            _const_spec((D_MODEL, PW_COLS)),
            _const_spec((1, LANES)),
        ],
        out_specs=[tok(D_INNER), tok(CONV_DIM), tok(D_MODEL), tok(2 * D_MODEL), tok(LANES)],
        out_shape=[
            jax.ShapeDtypeStruct((bsz, seq, D_INNER), BF16),
            jax.ShapeDtypeStruct((bsz, seq, CONV_DIM), BF16),
            jax.ShapeDtypeStruct((bsz, seq, D_MODEL), BF16),
            jax.ShapeDtypeStruct((bsz, seq, 2 * D_MODEL), BF16),
            jax.ShapeDtypeStruct((bsz, seq, LANES), F32),
        ],
        compiler_params=pltpu.CompilerParams(
            dimension_semantics=("parallel", "parallel"), vmem_limit_bytes=VMEM_LIMIT),
        name="mixer_inproj",
    )(h, mods, norm_w.reshape(1, D_MODEL), w, dtb)


def _tri_cumsum(tri, v):
    p1 = v.astype(BF16)
    r1 = v - p1.astype(F32)
    p2 = r1.astype(BF16)
    p3 = (r1 - p2.astype(F32)).astype(BF16)
    return _dot(tri, p1) + _dot(tri, p2) + _dot(tri, p3)


def _pair_bcast(v, c0, c1, left):
    rows = v.shape[0]
    a = jnp.broadcast_to(v[:, c0:c0 + 1], (rows, LANES))
    b = jnp.broadcast_to(v[:, c1:c1 + 1], (rows, LANES))
    return jnp.where(left, a, b)


def _ssd_conv(xm_ref, xp_ref, xn_ref, cw_ref, cb_ref, ext_ref, xc_ref, first, last):
    zero = jnp.zeros((HALO, CONV_DIM), F32)
    ext_ref[0:HALO, :] = jnp.where(first, zero, xp_ref[0].astype(F32))
    ext_ref[HALO:HALO + CHUNK, :] = xm_ref[0].astype(F32)
    ext_ref[HALO + CHUNK:2 * HALO + CHUNK, :] = jnp.where(last, zero, xn_ref[0].astype(F32))
    width = 2 * LANES
    for c in range(CONV_DIM // width):
        cols = slice(c * width, (c + 1) * width)
        acc = jnp.broadcast_to(cb_ref[:, cols], (CHUNK, width))
        for k in range(SSM_K):
            start = HALO - SSM_K // 2 + k
            acc = acc + cw_ref[k:k + 1, cols] * ext_ref[start:start + CHUNK, cols]
        xc_ref[:, cols] = _silu(acc)


def _ssd_chunk(xc_ref, dt, a_row, state_ref, y_ref, *, head_off, reverse):
    row = lax.broadcasted_iota(jnp.int32, (CHUNK, CHUNK), 0)
    col = lax.broadcasted_iota(jnp.int32, (CHUNK, CHUNK), 1)
    mask = (col >= row) if reverse else (row >= col)
    tri = mask.astype(BF16)
    left = lax.broadcasted_iota(jnp.int32, (CHUNK, LANES), 1) < HEADDIM
    left_row = lax.broadcasted_iota(jnp.int32, (1, LANES), 1) < HEADDIM

    da = dt * a_row
    cs = _tri_cumsum(tri, da)
    cs_t = cs.T
    tot = cs[0:1, :] if reverse else cs[CHUNK - 1:CHUNK, :]
    ecs = jnp.exp(cs)
    dtw = dt * jnp.exp(tot - cs)
    etot = jnp.exp(tot)

    for g in range(GROUPS):
        b_g = xc_ref[:, D_INNER + g * NSTATE:D_INNER + (g + 1) * NSTATE].astype(BF16)
        c_off = D_INNER + GROUPS * NSTATE + g * NSTATE
        c_g = xc_ref[:, c_off:c_off + NSTATE].astype(BF16)
        cb = lax.dot_general(c_g, b_g, (((1,), (1,)), ((), ())), preferred_element_type=F32)
        s_old = state_ref[g]
        y_off = _dot(c_g, s_old.astype(BF16))
        xw_parts = []
        dec_parts = []
        for p in range(HEADS_PER_GROUP // 2):
            h0 = g * HEADS_PER_GROUP + 2 * p
            c0, c1 = head_off + h0, head_off + h0 + 1
            lanes = slice(h0 * HEADDIM, (h0 + 2) * HEADDIM)
            x_pair = xc_ref[:, lanes]
            xs = (x_pair * _pair_bcast(dt, c0, c1, left)).astype(BF16)
            y_pair = []
            for cc in (c0, c1):
                diff = cs[:, cc:cc + 1] - cs_t[cc:cc + 1, :]
                lmat = jnp.exp(jnp.where(mask, diff, -jnp.inf))
                y_pair.append(_dot((cb * lmat).astype(BF16), xs))
            y_diag = jnp.where(left, y_pair[0], y_pair[1])
            y_ref[:, lanes] = y_diag + y_off[:, 2 * p * HEADDIM:(2 * p + 2) * HEADDIM] * _pair_bcast(ecs, c0, c1, left)
            xw_parts.append((x_pair * _pair_bcast(dtw, c0, c1, left)).astype(BF16))
            dec_parts.append(_pair_bcast(etot, c0, c1, left_row))
        xw = jnp.concatenate(xw_parts, axis=1)
        dec = jnp.concatenate(dec_parts, axis=1)
        s_new = lax.dot_general(b_g, xw, (((0,), (0,)), ((), ())), preferred_element_type=F32)
        state_ref[g] = s_old * dec + s_new


def _ssd_fwd_kernel(xm_ref, xp_ref, xn_ref, dt_ref, cw_ref, cb_ref, alog_ref,
                    y_ref, ext_ref, xc_ref, state_ref):
    j = pl.program_id(1)
    nj = pl.num_programs(1)

    @pl.when(j == 0)
    def _():
        state_ref[...] = jnp.zeros_like(state_ref)

    _ssd_conv(xm_ref, xp_ref, xn_ref, cw_ref, cb_ref, ext_ref, xc_ref, j == 0, j == nj - 1)
    a_row = -jnp.exp(alog_ref[...])
    _ssd_chunk(xc_ref, dt_ref[0], a_row, state_ref, y_ref.at[0], head_off=0, reverse=False)


def _ssd_bwd_kernel(xm_ref, xp_ref, xn_ref, dt_ref, cw_ref, cb_ref, alog_ref, yf_ref, zs_ref,
                    dskip_ref, nw_ref, o_ref, ext_ref, xc_ref, state_ref, y_ref):
    j = pl.program_id(1)
    nj = pl.num_programs(1)

    @pl.when(j == 0)
    def _():
        state_ref[...] = jnp.zeros_like(state_ref)

    _ssd_conv(xm_ref, xp_ref, xn_ref, cw_ref, cb_ref, ext_ref, xc_ref, j == nj - 1, j == 0)
    a_row = -jnp.exp(alog_ref[...])
    _ssd_chunk(xc_ref, dt_ref[0], a_row, state_ref, y_ref, head_off=HEADS, reverse=True)
    y = y_ref[...] + yf_ref[0] + dskip_ref[...] * xc_ref[:, 0:D_INNER]
    yg = y * zs_ref[0].astype(F32)
    yn = yg * lax.rsqrt(jnp.mean(yg * yg, axis=-1, keepdims=True) + EPS) * nw_ref[...]
    o_ref[0] = yn.astype(o_ref.dtype)


def _ssd(xbc, dt, zs, conv_w, conv_b, a_log, d_skip, norm_w):
    bsz, seq, _ = xbc.shape
    nc = seq // CHUNK
    hb = CHUNK // HALO
    nhb = seq // HALO
    pad = jnp.zeros((1, LANES - 2 * HEADS), F32)
    alog_row = jnp.concatenate([a_log.reshape(1, 2 * HEADS), pad], axis=1)
    cw = jnp.concatenate([conv_w, jnp.zeros((8 - SSM_K, CONV_DIM), F32)], axis=0)
    cb = conv_b.reshape(1, CONV_DIM)
    dskip_row = jnp.repeat(d_skip, HEADDIM).reshape(1, D_INNER)

    def specs(pos):
        return [
            pl.BlockSpec((1, CHUNK, CONV_DIM), lambda b, j: (b, pos(j), 0)),
            pl.BlockSpec((1, HALO, CONV_DIM), lambda b, j: (b, jnp.maximum(pos(j) * hb - 1, 0), 0)),
            pl.BlockSpec((1, HALO, CONV_DIM), lambda b, j: (b, jnp.minimum((pos(j) + 1) * hb, nhb - 1), 0)),
            pl.BlockSpec((1, CHUNK, LANES), lambda b, j: (b, pos(j), 0)),
            _const_spec((8, CONV_DIM)),
            _const_spec((1, CONV_DIM)),
            _const_spec((1, LANES)),
        ]

    scratch = [
        pltpu.VMEM((CHUNK + 2 * HALO, CONV_DIM), F32),
        pltpu.VMEM((CHUNK, CONV_DIM), F32),
        pltpu.VMEM((GROUPS, NSTATE, HEADS_PER_GROUP * HEADDIM), F32),
    ]
    params = pltpu.CompilerParams(
        dimension_semantics=("arbitrary", "arbitrary"), vmem_limit_bytes=VMEM_LIMIT)

    fwd_pos = lambda j: j
    y_fw = pl.pallas_call(
        _ssd_fwd_kernel,
        grid=(bsz, nc),
        in_specs=specs(fwd_pos),
        out_specs=pl.BlockSpec((1, CHUNK, D_INNER), lambda b, j: (b, j, 0)),
        out_shape=jax.ShapeDtypeStruct((bsz, seq, D_INNER), F32),
        scratch_shapes=scratch,
        compiler_params=params,
        name="ssd_forward",
    )(xbc, xbc, xbc, dt, cw, cb, alog_row)

    bwd_pos = lambda j: nc - 1 - j
    tok = lambda width: pl.BlockSpec((1, CHUNK, width), lambda b, j: (b, bwd_pos(j), 0))
    return pl.pallas_call(
        _ssd_bwd_kernel,
        grid=(bsz, nc),
        in_specs=specs(bwd_pos) + [tok(D_INNER), tok(D_INNER), _const_spec((1, D_INNER)),
                                   _const_spec((1, D_INNER))],
        out_specs=tok(D_INNER),
        out_shape=jax.ShapeDtypeStruct((bsz, seq, D_INNER), BF16),
        scratch_shapes=scratch + [pltpu.VMEM((CHUNK, D_INNER), F32)],
        compiler_params=params,
        name="ssd_backward",
    )(xbc, xbc, xbc, dt, cw, cb, alog_row, y_fw, zs, dskip_row, norm_w.reshape(1, D_INNER))


def _merge_kernel(h_ref, mod_ref, yn_ref, vm_ref, vp_ref, vn_ref, gates_ref, cw_ref, cb_ref, lng_ref, lnb_ref,
                  wpa_ref, wpb_ref, wo_ref, o_ref, ext_ref, vc_ref, *, tm):
    i = pl.program_id(1)
    ni = pl.num_programs(1)
    zero = jnp.zeros((HALO, D_MODEL), F32)
    ext_ref[0:HALO, :] = jnp.where(i == 0, zero, vp_ref[0].astype(F32))
    ext_ref[HALO:HALO + tm, :] = vm_ref[0].astype(F32)
    ext_ref[HALO + tm:2 * HALO + tm, :] = jnp.where(i == ni - 1, zero, vn_ref[0].astype(F32))
    width = 2 * LANES
    rows = 64
    for c in range(D_MODEL // width):
        cols = slice(c * width, (c + 1) * width)
        for r in range(tm // rows):
            acc = jnp.broadcast_to(cb_ref[:, cols], (rows, width))
            for k in range(CONF_K):
                start = HALO - CONF_K // 2 + k + r * rows
                acc = acc + cw_ref[k:k + 1, cols] * ext_ref[start:start + rows, cols]
            vc_ref[r * rows:(r + 1) * rows, cols] = acc
    vc = vc_ref[...]
    mu = jnp.mean(vc, axis=-1, keepdims=True)
    cen = vc - mu
    var = jnp.mean(cen * cen, axis=-1, keepdims=True)
    vb = _silu(cen * lax.rsqrt(var + EPS) * lng_ref[...] + lnb_ref[...]).astype(BF16)
    br_b = _dot(vb, wpb_ref[...])
    br_a = _dot(yn_ref[0], wpa_ref[...])
    g_a = gates_ref[0, :, 0:D_MODEL].astype(F32)
    g_b = gates_ref[0, :, D_MODEL:2 * D_MODEL].astype(F32)
    m = (g_a * br_a + g_b * br_b).astype(BF16)
    o_ref[0] = h_ref[0] + mod_ref[0, 5:6, :] * _dot(m, wo_ref[...])


def _merge(h, mods, yn, v, gates, conv_w, conv_b, ln_g, ln_b, w_pa, w_pb, w_out, *, tm=256):
    bsz, seq, _ = h.shape
    hb = tm // HALO
    nhb = seq // HALO
    cw = jnp.concatenate([conv_w, jnp.zeros((32 - CONF_K, D_MODEL), F32)], axis=0)

    def tok(width):
        return pl.BlockSpec((1, tm, width), lambda b, i: (b, i, 0))

    row = lambda a: a.reshape(1, D_MODEL)
    return pl.pallas_call(
        functools.partial(_merge_kernel, tm=tm),
        grid=(bsz, seq // tm),
        in_specs=[
            tok(D_MODEL),
            pl.BlockSpec((1, N_MOD, D_MODEL), lambda b, i: (b, 0, 0)),
            tok(D_INNER),
            tok(D_MODEL),
            pl.BlockSpec((1, HALO, D_MODEL), lambda b, i: (b, jnp.maximum(i * hb - 1, 0), 0)),
            pl.BlockSpec((1, HALO, D_MODEL), lambda b, i: (b, jnp.minimum((i + 1) * hb, nhb - 1), 0)),
            tok(2 * D_MODEL),
            _const_spec((32, D_MODEL)),
            _const_spec((1, D_MODEL)),
            _const_spec((1, D_MODEL)),
            _const_spec((1, D_MODEL)),
            _const_spec((D_INNER, D_MODEL)),
            _const_spec((D_MODEL, D_MODEL)),
            _const_spec((D_MODEL, D_MODEL)),
        ],
        out_specs=tok(D_MODEL),
        out_shape=jax.ShapeDtypeStruct(h.shape, F32),
        scratch_shapes=[pltpu.VMEM((tm + 2 * HALO, D_MODEL), F32), pltpu.VMEM((tm, D_MODEL), F32)],
        compiler_params=pltpu.CompilerParams(
            dimension_semantics=("parallel", "parallel"), vmem_limit_bytes=VMEM_LIMIT),
        name="mixer_merge",
    )(h, mods, yn, v, v, v, gates, cw, row(conv_b), row(ln_g), row(ln_b),
      w_pa.astype(BF16), w_pb.astype(BF16), w_out.astype(BF16))


def _trunk(x, mods_all, w):
    depth = w["w_in"].shape[0]
    h = x
    for l in range(depth):
        mods = mods_all[l]
        h = _ffn(h, mods, w["ffn1_norm"][l], w["ffn1_up"][l], w["ffn1_down"][l], w["final_norm"],
                 mod_base=0, final=False)
        zs, xbc, v, gates, dt = _inproj(h, mods, w["mix_norm"][l], w["w_in"][l], w["dt_bias"][l])
        yn = _ssd(xbc, dt, zs, w["ssm_conv_w"][l], w["ssm_conv_b"][l], w["a_log"][l], w["d_skip"][l],
                  w["ssm_norm"][l])
        h = _merge(h, mods, yn, v, gates, w["conf_conv_w"][l], w["conf_conv_b"][l], w["conf_ln_g"][l],
                   w["conf_ln_b"][l], w["w_proj_ssd"][l], w["w_proj_conv"][l], w["w_out"][l])
        h = _ffn(h, mods, w["ffn2_norm"][l], w["ffn2_up"][l], w["ffn2_down"][l], w["final_norm"],
                 mod_base=6, final=(l == depth - 1))
    return h


def kernel(x_prompt, x_sample, c_prompt, c_sample, w_ada, b_ada, ffn1_norm, ffn1_up, ffn1_down, mix_norm, w_in,
           ssm_conv_w, ssm_conv_b, dt_bias, a_log, d_skip, ssm_norm, w_proj_ssd, conf_conv_w, conf_conv_b,
           conf_ln_g, conf_ln_b, w_proj_conv, w_out, ffn2_norm, ffn2_up, ffn2_down, final_norm):
    w = dict(ffn1_norm=ffn1_norm, ffn1_up=ffn1_up, ffn1_down=ffn1_down, mix_norm=mix_norm, w_in=w_in,
             ssm_conv_w=ssm_conv_w, ssm_conv_b=ssm_conv_b, dt_bias=dt_bias, a_log=a_log, d_skip=d_skip,
             ssm_norm=ssm_norm, w_proj_ssd=w_proj_ssd, conf_conv_w=conf_conv_w, conf_conv_b=conf_conv_b,
             conf_ln_g=conf_ln_g, conf_ln_b=conf_ln_b, w_proj_conv=w_proj_conv, w_out=w_out,
             ffn2_norm=ffn2_norm, ffn2_up=ffn2_up, ffn2_down=ffn2_down, final_norm=final_norm)
    nb = x_prompt.shape[0]
    depth = w_ada.shape[0]
    c_all = jnp.concatenate([c_prompt, c_sample], axis=0)
    mods = _mods(c_all, w_ada, b_ada).reshape(depth, c_all.shape[0], N_MOD, D_MODEL)
    y_prompt = _trunk(x_prompt, mods[:, :nb], w)
    y_sample = _trunk(x_sample, mods[:, nb:], w)
    return (y_prompt, y_sample)
```

```python
import functools
import math

import jax
import jax.numpy as jnp
from jax import lax
from jax.experimental import pallas as pl
from jax.experimental.pallas import tpu as pltpu

D_MODEL = 1024
D_INNER = 2 * D_MODEL
HEADDIM = 64
HEADS = D_INNER // HEADDIM
GROUPS = 4
HEADS_PER_GROUP = HEADS // GROUPS
GROUP_COLS = HEADS_PER_GROUP * HEADDIM
NSTATE = 128
CHUNK = 128
SSM_K = 5
CONV_DIM = D_INNER + 2 * GROUPS * NSTATE
CONF_K = 31
FFN_DIM = 2816
N_MOD = 9
EPS = 1e-6
OFF_XBC = D_INNER
OFF_DT = OFF_XBC + CONV_DIM
OFF_GLU = OFF_DT + 2 * HEADS
OFF_GATE = OFF_GLU + 2 * D_MODEL
IN_COLS = OFF_GATE + 2 * D_MODEL

LANES = 128
SUBLANES = 8
HALO = 16
FFN_COLS = 256
PROJ_COLS = 512
VMEM_LIMIT = 56 * 1024 * 1024
LOG2E = math.log2(math.e)

F32 = jnp.float32
BF16 = jnp.bfloat16


def _dot(a, b):
    return jnp.dot(a, b, preferred_element_type=F32)


def _rms_mod(x, norm_w, shift, scale):
    y = x * lax.rsqrt(jnp.mean(x * x, axis=-1, keepdims=True) + EPS)
    return (y * norm_w) * (1.0 + scale) + shift


def _silu(x):
    return x * jax.nn.sigmoid(x)


def _const_spec(shape):
    nd = len(shape)
    return pl.BlockSpec(shape, lambda *_: (0,) * nd, pipeline_mode=pl.Buffered(1))


def _mods_kernel(c_ref, w_ref, b_ref, o_ref):
    c_act = _silu(c_ref[...])
    o_ref[0] = jnp.dot(c_act, w_ref[0], preferred_element_type=F32,
                       precision=lax.Precision.HIGHEST) + b_ref[0]


def _mods(c_all, w_ada, b_ada):
    depth, _, cols = w_ada.shape
    rows = c_all.shape[0]
    tn = D_MODEL
    return pl.pallas_call(
        _mods_kernel,
        grid=(depth, cols // tn),
        in_specs=[
            pl.BlockSpec((rows, D_MODEL), lambda l, j: (0, 0)),
            pl.BlockSpec((1, D_MODEL, tn), lambda l, j: (l, 0, j)),
            pl.BlockSpec((1, 1, tn), lambda l, j: (l, 0, j)),
        ],
        out_specs=pl.BlockSpec((1, rows, tn), lambda l, j: (l, 0, j)),
        out_shape=jax.ShapeDtypeStruct((depth, rows, cols), F32),
        name="adaln_mods",
    )(c_all, w_ada, b_ada.reshape(depth, 1, cols))


def _ffn_kernel(h_ref, mod_ref, nw_ref, wup_ref, wdn_ref, fnw_ref, o_ref, act_ref, *, mod_base, final):
    x = h_ref[0]
    shift = mod_ref[0, mod_base:mod_base + 1, :]
    scale = mod_ref[0, mod_base + 1:mod_base + 2, :]
    gate = mod_ref[0, mod_base + 2:mod_base + 3, :]
    u = _rms_mod(x, nw_ref[...], shift, scale).astype(BF16)
    for j in range(FFN_DIM // FFN_COLS):
        cols = slice(j * FFN_COLS, (j + 1) * FFN_COLS)
        gcols = slice(FFN_DIM + j * FFN_COLS, FFN_DIM + (j + 1) * FFN_COLS)
        a = _dot(u, wup_ref[:, cols])
        b = _dot(u, wup_ref[:, gcols])
        act_ref[:, cols] = (_silu(a) * b).astype(BF16)
    y = _dot(act_ref[...], wdn_ref[...])
    out = x + (0.5 * gate) * y
    if final:
        out = out * lax.rsqrt(jnp.mean(out * out, axis=-1, keepdims=True) + EPS) * fnw_ref[...]
    o_ref[0] = out


def _ffn(h, mods, norm_w, w_up, w_down, final_w, *, mod_base, final, tm=512):
    bsz, seq, _ = h.shape
    kern = functools.partial(_ffn_kernel, mod_base=mod_base, final=final)
    return pl.pallas_call(
        kern,
        grid=(bsz, seq // tm),
        in_specs=[
            pl.BlockSpec((1, tm, D_MODEL), lambda b, i: (b, i, 0)),
            pl.BlockSpec((1, N_MOD, D_MODEL), lambda b, i: (b, 0, 0)),
            _const_spec((1, D_MODEL)),
            _const_spec((D_MODEL, 2 * FFN_DIM)),
            _const_spec((FFN_DIM, D_MODEL)),
            _const_spec((1, D_MODEL)),
        ],
        out_specs=pl.BlockSpec((1, tm, D_MODEL), lambda b, i: (b, i, 0)),
        out_shape=jax.ShapeDtypeStruct(h.shape, F32),
        scratch_shapes=[pltpu.VMEM((tm, FFN_DIM), BF16)],
        compiler_params=pltpu.CompilerParams(
            dimension_semantics=("parallel", "parallel"), vmem_limit_bytes=VMEM_LIMIT),
        name="ffn_final" if final else "ffn",
    )(h, mods, norm_w.reshape(1, D_MODEL), w_up.astype(BF16), w_down.astype(BF16),
      final_w.reshape(1, D_MODEL))


PW_Z = 0
PW_XBC = PW_Z + D_INNER
PW_GLU = PW_XBC + CONV_DIM
PW_GATE = PW_GLU + 2 * D_MODEL
PW_DT = PW_GATE + 2 * D_MODEL
PW_COLS = PW_DT + LANES


def _inproj_kernel(h_ref, hp_ref, hn_ref, mod_ref, nw_ref, w_ref, dtb_ref, cw_ref, cb_ref,
                   zs_ref, xc_ref, v_ref, gates_ref, dt_ref, *, tm):
    i = pl.program_id(1)
    first = i == 0
    last = i == pl.num_programs(1) - 1
    hx = jnp.concatenate([hp_ref[0], h_ref[0], hn_ref[0]], axis=0)
    u_ext = _rms_mod(hx, nw_ref[...], mod_ref[0, 3:4, :], mod_ref[0, 4:5, :]).astype(BF16)
    u = u_ext[HALO:HALO + tm]
    for j in range(D_INNER // PROJ_COLS):
        cols = slice(j * PROJ_COLS, (j + 1) * PROJ_COLS)
        z = _dot(u, w_ref[:, PW_Z + j * PROJ_COLS:PW_Z + (j + 1) * PROJ_COLS])
        zs_ref[0, :, cols] = _silu(z).astype(BF16)
    zero = jnp.zeros((HALO, PROJ_COLS), F32)
    for j in range(CONV_DIM // PROJ_COLS):
        cols = slice(j * PROJ_COLS, (j + 1) * PROJ_COLS)
        p = _dot(u_ext, w_ref[:, PW_XBC + j * PROJ_COLS:PW_XBC + (j + 1) * PROJ_COLS])
        p = jnp.concatenate([jnp.where(first, zero, p[0:HALO]), p[HALO:HALO + tm],
                             jnp.where(last, zero, p[HALO + tm:])], axis=0)
        acc = jnp.broadcast_to(cb_ref[:, cols], (tm, PROJ_COLS))
        for k in range(SSM_K):
            start = HALO - SSM_K // 2 + k
            acc = acc + cw_ref[k:k + 1, cols] * p[start:start + tm]
        xc_ref[0, :, cols] = _silu(acc).astype(BF16)
    for j in range(D_MODEL // PROJ_COLS):
        cols = slice(j * PROJ_COLS, (j + 1) * PROJ_COLS)
        va = _dot(u, w_ref[:, PW_GLU + j * PROJ_COLS:PW_GLU + (j + 1) * PROJ_COLS])
        vg = _dot(u, w_ref[:, PW_GLU + D_MODEL + j * PROJ_COLS:PW_GLU + D_MODEL + (j + 1) * PROJ_COLS])
        v_ref[0, :, cols] = (va * jax.nn.sigmoid(vg)).astype(BF16)
    for j in range(2 * D_MODEL // PROJ_COLS):
        cols = slice(j * PROJ_COLS, (j + 1) * PROJ_COLS)
        gt = _dot(u, w_ref[:, PW_GATE + j * PROJ_COLS:PW_GATE + (j + 1) * PROJ_COLS])
        gates_ref[0, :, cols] = jax.nn.sigmoid(gt).astype(BF16)
    dtr = _dot(u, w_ref[:, PW_DT:PW_DT + LANES]) + dtb_ref[...]
    dt_ref[0] = jnp.maximum(dtr, 0.0) + jnp.log(1.0 + jnp.exp(-jnp.abs(dtr)))


def _inproj(h, mods, norm_w, w_in, dt_bias, conv_w, conv_b, *, tm=512):
    bsz, seq, _ = h.shape
    hb = tm // HALO
    nhb = seq // HALO
    w = jnp.concatenate([
        w_in[:, :OFF_DT], w_in[:, OFF_GLU:], w_in[:, OFF_DT:OFF_GLU],
        jnp.zeros((D_MODEL, LANES - 2 * HEADS), w_in.dtype)], axis=1).astype(BF16)
    dtb = jnp.concatenate([dt_bias.reshape(1, 2 * HEADS), jnp.zeros((1, LANES - 2 * HEADS), F32)], axis=1)
    cw = jnp.concatenate([conv_w, jnp.zeros((SUBLANES - SSM_K, CONV_DIM), F32)], axis=0)

    def tok(width):
        return pl.BlockSpec((1, tm, width), lambda b, i: (b, i, 0))

    return pl.pallas_call(
        functools.partial(_inproj_kernel, tm=tm),
        grid=(bsz, seq // tm),
        in_specs=[
            tok(D_MODEL),
            pl.BlockSpec((1, HALO, D_MODEL), lambda b, i: (b, jnp.maximum(i * hb - 1, 0), 0)),
            pl.BlockSpec((1, HALO, D_MODEL), lambda b, i: (b, jnp.minimum((i + 1) * hb, nhb - 1), 0)),
            pl.BlockSpec((1, N_MOD, D_MODEL), lambda b, i: (b, 0, 0)),
            _const_spec((1, D_MODEL)),
            _const_spec((D_MODEL, PW_COLS)),
            _const_spec((1, LANES)),
            _const_spec((SUBLANES, CONV_DIM)),
            _const_spec((1, CONV_DIM)),
        ],
        out_specs=[tok(D_INNER), tok(CONV_DIM), tok(D_MODEL), tok(2 * D_MODEL), tok(LANES)],
        out_shape=[
            jax.ShapeDtypeStruct((bsz, seq, D_INNER), BF16),
            jax.ShapeDtypeStruct((bsz, seq, CONV_DIM), BF16),
            jax.ShapeDtypeStruct((bsz, seq, D_MODEL), BF16),
            jax.ShapeDtypeStruct((bsz, seq, 2 * D_MODEL), BF16),
            jax.ShapeDtypeStruct((bsz, seq, LANES), F32),
        ],
        compiler_params=pltpu.CompilerParams(
            dimension_semantics=("parallel", "parallel"), vmem_limit_bytes=VMEM_LIMIT),
        name="mixer_inproj",
    )(h, h, h, mods, norm_w.reshape(1, D_MODEL), w, dtb, cw, conv_b.reshape(1, CONV_DIM))


def _split2(v):
    hi = v.astype(BF16)
    lo = (v - hi.astype(F32)).astype(BF16)
    return jnp.concatenate([hi, lo], axis=1)


def _tri_cumsum(tri, v):
    p1 = v.astype(BF16)
    r1 = v - p1.astype(F32)
    p2 = r1.astype(BF16)
    p3 = (r1 - p2.astype(F32)).astype(BF16)
    return _dot(tri, p1) + _dot(tri, p2) + _dot(tri, p3)


def _ssd_chunk(xc_ref, dt_ref, e2_ref, state_ref, rows, a_row, mask, tri, left, *, reverse, head_off):
    dt = dt_ref[0, rows, :]
    cs2 = _tri_cumsum(tri, dt * a_row) * LOG2E
    tot2 = cs2[0:1, :] if reverse else cs2[CHUNK - 1:CHUNK, :]
    src_t = (jnp.log(dt) * LOG2E - cs2).T
    ecs = jnp.exp2(cs2)
    dtw = dt * jnp.exp2(tot2 - cs2)
    etot = jnp.broadcast_to(jnp.exp2(tot2), (SUBLANES, LANES))
    wide = _dot(jnp.concatenate([_split2(ecs), _split2(dtw), _split2(etot)], axis=0), e2_ref[...])
    ecs_w = wide[0:CHUNK]
    dtw_w = wide[CHUNK:2 * CHUNK].astype(BF16)
    etot_w = wide[2 * CHUNK:2 * CHUNK + 1]

    y_groups = []
    for g in range(GROUPS):
        gcols = slice(g * GROUP_COLS, (g + 1) * GROUP_COLS)
        b_g = xc_ref[0, rows, D_INNER + g * NSTATE:D_INNER + (g + 1) * NSTATE]
        c_off = D_INNER + GROUPS * NSTATE + g * NSTATE
        c_g = xc_ref[0, rows, c_off:c_off + NSTATE]
        cb = lax.dot_general(c_g, b_g, (((1,), (1,)), ((), ())), preferred_element_type=F32)
        s_old = state_ref[g]
        y_off = _dot(c_g, s_old.astype(BF16)) * ecs_w[:, gcols]
        y_pairs = []
        for p in range(HEADS_PER_GROUP // 2):
            h0 = g * HEADS_PER_GROUP + 2 * p
            x_pair = xc_ref[0, rows, h0 * HEADDIM:(h0 + 2) * HEADDIM]
            mats = []
            for cc in (head_off + h0, head_off + h0 + 1):
                arg = cs2[:, cc:cc + 1] + src_t[cc:cc + 1, :]
                mats.append((cb * jnp.exp2(jnp.where(mask, arg, -jnp.inf))).astype(BF16))
            zero = jnp.zeros_like(x_pair)
            rhs = jnp.concatenate([jnp.where(left, x_pair, zero), jnp.where(left, zero, x_pair)], axis=0)
            y_pairs.append(_dot(jnp.concatenate(mats, axis=1), rhs))
        y_groups.append(jnp.concatenate(y_pairs, axis=1) + y_off)
        xw = xc_ref[0, rows, gcols] * dtw_w[:, gcols]
        s_new = lax.dot_general(b_g, xw, (((0,), (0,)), ((), ())), preferred_element_type=F32)
        state_ref[g] = s_old * etot_w[:, gcols] + s_new
    return jnp.concatenate(y_groups, axis=1)


def _ssd_setup(alog_ref, state_ref, *, reverse):
    @pl.when(pl.program_id(1) == 0)
    def _():
        state_ref[...] = jnp.zeros_like(state_ref)

    row = lax.broadcasted_iota(jnp.int32, (CHUNK, CHUNK), 0)
    col = lax.broadcasted_iota(jnp.int32, (CHUNK, CHUNK), 1)
    mask = (col >= row) if reverse else (row >= col)
    left = lax.broadcasted_iota(jnp.int32, (CHUNK, LANES), 1) < HEADDIM
    return -jnp.exp(alog_ref[...]), mask, mask.astype(BF16), left


def _ssd_fwd_kernel(xc_ref, dt_ref, alog_ref, e2_ref, y_ref, state_ref, *, nchunks):
    a_row, mask, tri, left = _ssd_setup(alog_ref, state_ref, reverse=False)

    def body(c, carry):
        rows = pl.ds(pl.multiple_of(c * CHUNK, CHUNK), CHUNK)
        y = _ssd_chunk(xc_ref, dt_ref, e2_ref, state_ref, rows, a_row, mask, tri, left,
                       reverse=False, head_off=0)
        y_ref[0, rows, :] = y.astype(y_ref.dtype)
        return carry

    lax.fori_loop(0, nchunks, body, 0)


def _ssd_bwd_kernel(xc_ref, dt_ref, alog_ref, e2_ref, yf_ref, zs_ref, dskip_ref, nw_ref, o_ref, state_ref,
                    *, nchunks):
    a_row, mask, tri, left = _ssd_setup(alog_ref, state_ref, reverse=True)

    def body(i, carry):
        rows = pl.ds(pl.multiple_of((nchunks - 1 - i) * CHUNK, CHUNK), CHUNK)
        y = _ssd_chunk(xc_ref, dt_ref, e2_ref, state_ref, rows, a_row, mask, tri, left,
                       reverse=True, head_off=HEADS)
        y = y + yf_ref[0, rows, :].astype(F32) + dskip_ref[...] * xc_ref[0, rows, 0:D_INNER].astype(F32)
        yg = y * zs_ref[0, rows, :].astype(F32)
        yn = yg * lax.rsqrt(jnp.mean(yg * yg, axis=-1, keepdims=True) + EPS) * nw_ref[...]
        o_ref[0, rows, :] = yn.astype(o_ref.dtype)
        return carry

    lax.fori_loop(0, nchunks, body, 0)


def _expansion(head_off):
    r = lax.broadcasted_iota(jnp.int32, (LANES, D_INNER), 0)
    c = lax.broadcasted_iota(jnp.int32, (LANES, D_INNER), 1)
    e = (r == head_off + c // HEADDIM).astype(BF16)
    return jnp.concatenate([e, e], axis=0)


def _ssd(xc, dt, zs, a_log, d_skip, norm_w, *, tb=512):
    bsz, seq, _ = xc.shape
    nb = seq // tb
    nchunks = tb // CHUNK
    pad = jnp.zeros((1, LANES - 2 * HEADS), F32)
    alog_row = jnp.concatenate([a_log.reshape(1, 2 * HEADS), pad], axis=1)
    dskip_row = jnp.repeat(d_skip, HEADDIM).reshape(1, D_INNER)

    def specs(pos):
        return [
            pl.BlockSpec((1, tb, CONV_DIM), lambda b, j: (b, pos(j), 0)),
            pl.BlockSpec((1, tb, LANES), lambda b, j: (b, pos(j), 0)),
            _const_spec((1, LANES)),
            _const_spec((2 * LANES, D_INNER)),
        ]

    scratch = [pltpu.VMEM((GROUPS, NSTATE, GROUP_COLS), F32)]
    params = pltpu.CompilerParams(
        dimension_semantics=("arbitrary", "arbitrary"), vmem_limit_bytes=VMEM_LIMIT)

    y_fw = pl.pallas_call(
        functools.partial(_ssd_fwd_kernel, nchunks=nchunks),
        grid=(bsz, nb),
        in_specs=specs(lambda j: j),
        out_specs=pl.BlockSpec((1, tb, D_INNER), lambda b, j: (b, j, 0)),
        out_shape=jax.ShapeDtypeStruct((bsz, seq, D_INNER), BF16),
        scratch_shapes=scratch,
        compiler_params=params,
        name="ssd_forward",
    )(xc, dt, alog_row, _expansion(0))

    bwd_pos = lambda j: nb - 1 - j
    tok = lambda width: pl.BlockSpec((1, tb, width), lambda b, j: (b, bwd_pos(j), 0))
    return pl.pallas_call(
        functools.partial(_ssd_bwd_kernel, nchunks=nchunks),
        grid=(bsz, nb),
        in_specs=specs(bwd_pos) + [tok(D_INNER), tok(D_INNER), _const_spec((1, D_INNER)),
                                   _const_spec((1, D_INNER))],
        out_specs=tok(D_INNER),
        out_shape=jax.ShapeDtypeStruct((bsz, seq, D_INNER), BF16),
        scratch_shapes=scratch,
        compiler_params=params,
        name="ssd_backward",
    )(xc, dt, alog_row, _expansion(HEADS), y_fw, zs, dskip_row, norm_w.reshape(1, D_INNER))


CONF_ROWS = 64
CONF_COLS = 2 * LANES


def _merge_kernel(h_ref, mod_ref, yn_ref, vm_ref, vp_ref, vn_ref, gates_ref, cw_ref, cb_ref, lng_ref, lnb_ref,
                  wpa_ref, wpb_ref, wo_ref, o_ref, sh_ref, vc_ref, *, tm):
    i = pl.program_id(1)
    ni = pl.num_programs(1)
    ext = tm + 2 * HALO
    zero = jnp.zeros((HALO, D_MODEL), F32)
    sh_ref[0, 0:HALO, :] = jnp.where(i == 0, zero, vp_ref[0].astype(F32))
    sh_ref[0, HALO:HALO + tm, :] = vm_ref[0].astype(F32)
    sh_ref[0, HALO + tm:ext, :] = jnp.where(i == ni - 1, zero, vn_ref[0].astype(F32))
    for r in range(1, SUBLANES):
        sh_ref[r, 0:ext - SUBLANES, :] = sh_ref[0, r:r + ext - SUBLANES, :]
    for c in range(D_MODEL // CONF_COLS):
        cols = slice(c * CONF_COLS, (c + 1) * CONF_COLS)
        for rb in range(tm // CONF_ROWS):
            acc = jnp.broadcast_to(cb_ref[:, cols], (CONF_ROWS, CONF_COLS))
            for k in range(CONF_K):
                off = HALO - CONF_K // 2 + k
                start = rb * CONF_ROWS + (off // SUBLANES) * SUBLANES
                acc = acc + cw_ref[k:k + 1, cols] * sh_ref[off % SUBLANES, start:start + CONF_ROWS, cols]
            vc_ref[rb * CONF_ROWS:(rb + 1) * CONF_ROWS, cols] = acc
    vc = vc_ref[...]
    mu = jnp.mean(vc, axis=-1, keepdims=True)
    cen = vc - mu
    var = jnp.mean(cen * cen, axis=-1, keepdims=True)
    vb = _silu(cen * lax.rsqrt(var + EPS) * lng_ref[...] + lnb_ref[...]).astype(BF16)
    br_b = _dot(vb, wpb_ref[...])
    br_a = _dot(yn_ref[0], wpa_ref[...])
    g_a = gates_ref[0, :, 0:D_MODEL].astype(F32)
    g_b = gates_ref[0, :, D_MODEL:2 * D_MODEL].astype(F32)
    m = (g_a * br_a + g_b * br_b).astype(BF16)
    o_ref[0] = h_ref[0] + mod_ref[0, 5:6, :] * _dot(m, wo_ref[...])


def _merge(h, mods, yn, v, gates, conv_w, conv_b, ln_g, ln_b, w_pa, w_pb, w_out, *, tm=256):
    bsz, seq, _ = h.shape
    hb = tm // HALO
    nhb = seq // HALO
    cw = jnp.concatenate([conv_w, jnp.zeros((32 - CONF_K, D_MODEL), F32)], axis=0)

    def tok(width):
        return pl.BlockSpec((1, tm, width), lambda b, i: (b, i, 0))

    row = lambda a: a.reshape(1, D_MODEL)
    return pl.pallas_call(
        functools.partial(_merge_kernel, tm=tm),
        grid=(bsz, seq // tm),
        in_specs=[
            tok(D_MODEL),
            pl.BlockSpec((1, N_MOD, D_MODEL), lambda b, i: (b, 0, 0)),
            tok(D_INNER),
            tok(D_MODEL),
            pl.BlockSpec((1, HALO, D_MODEL), lambda b, i: (b, jnp.maximum(i * hb - 1, 0), 0)),
            pl.BlockSpec((1, HALO, D_MODEL), lambda b, i: (b, jnp.minimum((i + 1) * hb, nhb - 1), 0)),
            tok(2 * D_MODEL),
            _const_spec((32, D_MODEL)),
            _const_spec((1, D_MODEL)),
            _const_spec((1, D_MODEL)),
            _const_spec((1, D_MODEL)),
            _const_spec((D_INNER, D_MODEL)),
            _const_spec((D_MODEL, D_MODEL)),
            _const_spec((D_MODEL, D_MODEL)),
        ],
        out_specs=tok(D_MODEL),
        out_shape=jax.ShapeDtypeStruct(h.shape, F32),
        scratch_shapes=[pltpu.VMEM((SUBLANES, tm + 2 * HALO, D_MODEL), F32), pltpu.VMEM((tm, D_MODEL), F32)],
        compiler_params=pltpu.CompilerParams(
            dimension_semantics=("parallel", "parallel"), vmem_limit_bytes=VMEM_LIMIT),
        name="mixer_merge",
    )(h, mods, yn, v, v, v, gates, cw, row(conv_b), row(ln_g), row(ln_b),
      w_pa.astype(BF16), w_pb.astype(BF16), w_out.astype(BF16))


def _trunk(x, mods_all, w):
    depth = w["w_in"].shape[0]
    h = x
    for l in range(depth):
        mods = mods_all[l]
        h = _ffn(h, mods, w["ffn1_norm"][l], w["ffn1_up"][l], w["ffn1_down"][l], w["final_norm"],
                 mod_base=0, final=False)
        zs, xc, v, gates, dt = _inproj(h, mods, w["mix_norm"][l], w["w_in"][l], w["dt_bias"][l],
                                       w["ssm_conv_w"][l], w["ssm_conv_b"][l])
        yn = _ssd(xc, dt, zs, w["a_log"][l], w["d_skip"][l], w["ssm_norm"][l])
        h = _merge(h, mods, yn, v, gates, w["conf_conv_w"][l], w["conf_conv_b"][l], w["conf_ln_g"][l],
                   w["conf_ln_b"][l], w["w_proj_ssd"][l], w["w_proj_conv"][l], w["w_out"][l])
        h = _ffn(h, mods, w["ffn2_norm"][l], w["ffn2_up"][l], w["ffn2_down"][l], w["final_norm"],
                 mod_base=6, final=(l == depth - 1))
    return h


def kernel(x_prompt, x_sample, c_prompt, c_sample, w_ada, b_ada, ffn1_norm, ffn1_up, ffn1_down, mix_norm, w_in,
           ssm_conv_w, ssm_conv_b, dt_bias, a_log, d_skip, ssm_norm, w_proj_ssd, conf_conv_w, conf_conv_b,
           conf_ln_g, conf_ln_b, w_proj_conv, w_out, ffn2_norm, ffn2_up, ffn2_down, final_norm):
    w = dict(ffn1_norm=ffn1_norm, ffn1_up=ffn1_up, ffn1_down=ffn1_down, mix_norm=mix_norm, w_in=w_in,
             ssm_conv_w=ssm_conv_w, ssm_conv_b=ssm_conv_b, dt_bias=dt_bias, a_log=a_log, d_skip=d_skip,
             ssm_norm=ssm_norm, w_proj_ssd=w_proj_ssd, conf_conv_w=conf_conv_w, conf_conv_b=conf_conv_b,
             conf_ln_g=conf_ln_g, conf_ln_b=conf_ln_b, w_proj_conv=w_proj_conv, w_out=w_out,
             ffn2_norm=ffn2_norm, ffn2_up=ffn2_up, ffn2_down=ffn2_down, final_norm=final_norm)
    nb = x_prompt.shape[0]
    depth = w_ada.shape[0]
    c_all = jnp.concatenate([c_prompt, c_sample], axis=0)
    mods = _mods(c_all, w_ada, b_ada).reshape(depth, c_all.shape[0], N_MOD, D_MODEL)
    y_prompt = _trunk(x_prompt, mods[:, :nb], w)
    y_sample = _trunk(x_sample, mods[:, nb:], w)
    return (y_prompt, y_sample)
```

```python
import functools
import math

import jax
import jax.numpy as jnp
from jax import lax
from jax.experimental import pallas as pl
from jax.experimental.pallas import tpu as pltpu

D_MODEL = 1024
D_INNER = 2 * D_MODEL
HEADDIM = 64
HEADS = D_INNER // HEADDIM
GROUPS = 4
HEADS_PER_GROUP = HEADS // GROUPS
GROUP_COLS = HEADS_PER_GROUP * HEADDIM
NSTATE = 128
CHUNK = 128
SSM_K = 5
CONV_DIM = D_INNER + 2 * GROUPS * NSTATE
CONF_K = 31
FFN_DIM = 2816
N_MOD = 9
EPS = 1e-6
OFF_XBC = D_INNER
OFF_DT = OFF_XBC + CONV_DIM
OFF_GLU = OFF_DT + 2 * HEADS
OFF_GATE = OFF_GLU + 2 * D_MODEL
IN_COLS = OFF_GATE + 2 * D_MODEL

LANES = 128
SUBLANES = 8
HALO = 16
FFN_COLS = 256
PROJ_COLS = 512
CONV_ROWS = 64
CONV_COLS = 2 * LANES
VMEM_LIMIT = 56 * 1024 * 1024
LOG2E = math.log2(math.e)

F32 = jnp.float32
BF16 = jnp.bfloat16


def _dot(a, b):
    return jnp.dot(a, b, preferred_element_type=F32)


def _rms_mod(x, norm_w, shift, scale):
    y = x * lax.rsqrt(jnp.mean(x * x, axis=-1, keepdims=True) + EPS)
    return (y * norm_w) * (1.0 + scale) + shift


def _silu(x):
    return x * jax.nn.sigmoid(x)


def _row_windows(tiles, n_out, offsets):
    width = tiles[0].shape[1]
    sub = lax.broadcasted_iota(jnp.int32, (SUBLANES, width), 0)
    rolled = {}
    out = {}
    for o in offsets:
        q, r = divmod(o, SUBLANES)
        if r == 0:
            out[o] = jnp.concatenate(tiles[q:q + n_out], axis=0)
            continue
        if r not in rolled:
            rolled[r] = [pltpu.roll(t, SUBLANES - r, axis=0) for t in tiles]
        from_next = sub >= SUBLANES - r
        rt = rolled[r]
        out[o] = jnp.concatenate(
            [jnp.where(from_next, rt[q + i + 1], rt[q + i]) for i in range(n_out)], axis=0)
    return out


def _const_spec(shape):
    nd = len(shape)
    return pl.BlockSpec(shape, lambda *_: (0,) * nd, pipeline_mode=pl.Buffered(1))


def _mods_kernel(c_ref, w_ref, b_ref, o_ref):
    c_act = _silu(c_ref[...])
    o_ref[0] = jnp.dot(c_act, w_ref[0], preferred_element_type=F32,
                       precision=lax.Precision.HIGHEST) + b_ref[0]


def _mods(c_all, w_ada, b_ada):
    depth, _, cols = w_ada.shape
    rows = c_all.shape[0]
    tn = D_MODEL
    return pl.pallas_call(
        _mods_kernel,
        grid=(depth, cols // tn),
        in_specs=[
            pl.BlockSpec((rows, D_MODEL), lambda l, j: (0, 0)),
            pl.BlockSpec((1, D_MODEL, tn), lambda l, j: (l, 0, j)),
            pl.BlockSpec((1, 1, tn), lambda l, j: (l, 0, j)),
        ],
        out_specs=pl.BlockSpec((1, rows, tn), lambda l, j: (l, 0, j)),
        out_shape=jax.ShapeDtypeStruct((depth, rows, cols), F32),
        name="adaln_mods",
    )(c_all, w_ada, b_ada.reshape(depth, 1, cols))


def _ffn_kernel(h_ref, mod_ref, nw_ref, wup_ref, wdn_ref, fnw_ref, o_ref, act_ref, *, mod_base, final):
    x = h_ref[0]
    shift = mod_ref[0, mod_base:mod_base + 1, :]
    scale = mod_ref[0, mod_base + 1:mod_base + 2, :]
    gate = mod_ref[0, mod_base + 2:mod_base + 3, :]
    u = _rms_mod(x, nw_ref[...], shift, scale).astype(BF16)
    for j in range(FFN_DIM // FFN_COLS):
        cols = slice(j * FFN_COLS, (j + 1) * FFN_COLS)
        gcols = slice(FFN_DIM + j * FFN_COLS, FFN_DIM + (j + 1) * FFN_COLS)
        a = _dot(u, wup_ref[:, cols])
        b = _dot(u, wup_ref[:, gcols])
        act_ref[:, cols] = (_silu(a) * b).astype(BF16)
    y = _dot(act_ref[...], wdn_ref[...])
    out = x + (0.5 * gate) * y
    if final:
        out = out * lax.rsqrt(jnp.mean(out * out, axis=-1, keepdims=True) + EPS) * fnw_ref[...]
    o_ref[0] = out


def _ffn(h, mods, norm_w, w_up, w_down, final_w, *, mod_base, final, tm=512):
    bsz, seq, _ = h.shape
    kern = functools.partial(_ffn_kernel, mod_base=mod_base, final=final)
    return pl.pallas_call(
        kern,
        grid=(bsz, seq // tm),
        in_specs=[
            pl.BlockSpec((1, tm, D_MODEL), lambda b, i: (b, i, 0)),
            pl.BlockSpec((1, N_MOD, D_MODEL), lambda b, i: (b, 0, 0)),
            _const_spec((1, D_MODEL)),
            _const_spec((D_MODEL, 2 * FFN_DIM)),
            _const_spec((FFN_DIM, D_MODEL)),
            _const_spec((1, D_MODEL)),
        ],
        out_specs=pl.BlockSpec((1, tm, D_MODEL), lambda b, i: (b, i, 0)),
        out_shape=jax.ShapeDtypeStruct(h.shape, F32),
        scratch_shapes=[pltpu.VMEM((tm, FFN_DIM), BF16)],
        compiler_params=pltpu.CompilerParams(
            dimension_semantics=("parallel", "parallel"), vmem_limit_bytes=VMEM_LIMIT),
        name="ffn_final" if final else "ffn",
    )(h, mods, norm_w.reshape(1, D_MODEL), w_up.astype(BF16), w_down.astype(BF16),
      final_w.reshape(1, D_MODEL))


PW_Z = 0
PW_XBC = PW_Z + D_INNER
PW_GLU = PW_XBC + CONV_DIM
PW_GATE = PW_GLU + 2 * D_MODEL
PW_DT = PW_GATE + 2 * D_MODEL
PW_COLS = PW_DT + LANES


def _inproj_kernel(h_ref, hp_ref, hn_ref, mod_ref, nw_ref, w_ref, dtb_ref, cw_ref, cb_ref,
                   zs_ref, xc_ref, v_ref, gates_ref, dt_ref, u_ref, p_ref, *, tm):
    i = pl.program_id(1)
    first = i == 0
    last = i == pl.num_programs(1) - 1
    hx = jnp.concatenate([hp_ref[0], h_ref[0], hn_ref[0]], axis=0)
    u_ref[...] = _rms_mod(hx, nw_ref[...], mod_ref[0, 3:4, :], mod_ref[0, 4:5, :]).astype(BF16)
    zero = jnp.zeros((HALO, PROJ_COLS), F32)
    u_ext = slice(0, tm + 2 * HALO)
    u = slice(HALO, HALO + tm)

    def proj(rows, base, j):
        return _dot(u_ref[rows, :], w_ref[:, base + j * PROJ_COLS:base + (j + 1) * PROJ_COLS])

    def z_unit(j):
        zs_ref[0, :, j * PROJ_COLS:(j + 1) * PROJ_COLS] = _silu(proj(u, PW_Z, j)).astype(BF16)

    def conv_unit(j):
        p = proj(u_ext, PW_XBC, j)
        p_ref[0:HALO, :] = jnp.where(first, zero, p[0:HALO])
        p_ref[HALO:HALO + tm, :] = p[HALO:HALO + tm]
        p_ref[HALO + tm:, :] = jnp.where(last, zero, p[HALO + tm:])
        for c in range(PROJ_COLS // CONV_COLS):
            pc = slice(c * CONV_COLS, (c + 1) * CONV_COLS)
            cols = slice(j * PROJ_COLS + c * CONV_COLS, j * PROJ_COLS + (c + 1) * CONV_COLS)
            for rb in range(tm // CONV_ROWS):
                acc = jnp.broadcast_to(cb_ref[:, cols], (CONV_ROWS, CONV_COLS))
                for k in range(SSM_K):
                    start = HALO - SSM_K // 2 + k + rb * CONV_ROWS
                    acc = acc + cw_ref[k:k + 1, cols] * p_ref[start:start + CONV_ROWS, pc]
                xc_ref[0, rb * CONV_ROWS:(rb + 1) * CONV_ROWS, cols] = _silu(acc).astype(BF16)

    def glu_unit(j):
        va = proj(u, PW_GLU, j)
        vg = proj(u, PW_GLU + D_MODEL, j)
        v_ref[0, :, j * PROJ_COLS:(j + 1) * PROJ_COLS] = (va * jax.nn.sigmoid(vg)).astype(BF16)

    def gate_unit(j):
        gates_ref[0, :, j * PROJ_COLS:(j + 1) * PROJ_COLS] = jax.nn.sigmoid(proj(u, PW_GATE, j)).astype(BF16)

    light = ([functools.partial(z_unit, j) for j in range(D_INNER // PROJ_COLS)]
             + [functools.partial(glu_unit, j) for j in range(D_MODEL // PROJ_COLS)]
             + [functools.partial(gate_unit, j) for j in range(2 * D_MODEL // PROJ_COLS)])
    n_conv = CONV_DIM // PROJ_COLS
    for j in range(n_conv):
        conv_unit(j)
        for unit in light[j * len(light) // n_conv:(j + 1) * len(light) // n_conv]:
            unit()
    dtr = _dot(u_ref[HALO:HALO + tm, :], w_ref[:, PW_DT:PW_DT + LANES]) + dtb_ref[...]
    dt_ref[0] = jnp.maximum(dtr, 0.0) + jnp.log(1.0 + jnp.exp(-jnp.abs(dtr)))


def _inproj(h, mods, norm_w, w_in, dt_bias, conv_w, conv_b, *, tm=512):
    bsz, seq, _ = h.shape
    hb = tm // HALO
    nhb = seq // HALO
    w = jnp.concatenate([
        w_in[:, :OFF_DT], w_in[:, OFF_GLU:], w_in[:, OFF_DT:OFF_GLU],
        jnp.zeros((D_MODEL, LANES - 2 * HEADS), w_in.dtype)], axis=1).astype(BF16)
    dtb = jnp.concatenate([dt_bias.reshape(1, 2 * HEADS), jnp.zeros((1, LANES - 2 * HEADS), F32)], axis=1)
    cw = jnp.concatenate([conv_w, jnp.zeros((SUBLANES - SSM_K, CONV_DIM), F32)], axis=0)

    def tok(width):
        return pl.BlockSpec((1, tm, width), lambda b, i: (b, i, 0))

    return pl.pallas_call(
        functools.partial(_inproj_kernel, tm=tm),
        grid=(bsz, seq // tm),
        in_specs=[
            tok(D_MODEL),
            pl.BlockSpec((1, HALO, D_MODEL), lambda b, i: (b, jnp.maximum(i * hb - 1, 0), 0)),
            pl.BlockSpec((1, HALO, D_MODEL), lambda b, i: (b, jnp.minimum((i + 1) * hb, nhb - 1), 0)),
            pl.BlockSpec((1, N_MOD, D_MODEL), lambda b, i: (b, 0, 0)),
            _const_spec((1, D_MODEL)),
            _const_spec((D_MODEL, PW_COLS)),
            _const_spec((1, LANES)),
            _const_spec((SUBLANES, CONV_DIM)),
            _const_spec((1, CONV_DIM)),
        ],
        out_specs=[tok(D_INNER), tok(CONV_DIM), tok(D_MODEL), tok(2 * D_MODEL), tok(LANES)],
        out_shape=[
            jax.ShapeDtypeStruct((bsz, seq, D_INNER), BF16),
            jax.ShapeDtypeStruct((bsz, seq, CONV_DIM), BF16),
            jax.ShapeDtypeStruct((bsz, seq, D_MODEL), BF16),
            jax.ShapeDtypeStruct((bsz, seq, 2 * D_MODEL), BF16),
            jax.ShapeDtypeStruct((bsz, seq, LANES), F32),
        ],
        scratch_shapes=[pltpu.VMEM((tm + 2 * HALO, D_MODEL), BF16),
                        pltpu.VMEM((tm + 2 * HALO, PROJ_COLS), F32)],
        compiler_params=pltpu.CompilerParams(
            dimension_semantics=("parallel", "parallel"), vmem_limit_bytes=VMEM_LIMIT),
        name="mixer_inproj",
    )(h, h, h, mods, norm_w.reshape(1, D_MODEL), w, dtb, cw, conv_b.reshape(1, CONV_DIM))


def _split2(v):
    hi = v.astype(BF16)
    lo = (v - hi.astype(F32)).astype(BF16)
    return jnp.concatenate([hi, lo], axis=1)


def _tri_cumsum(tri, v):
    p1 = v.astype(BF16)
    r1 = v - p1.astype(F32)
    p2 = r1.astype(BF16)
    p3 = (r1 - p2.astype(F32)).astype(BF16)
    return _dot(tri, p1) + _dot(tri, p2) + _dot(tri, p3)


def _ssd_chunk(xc_ref, dt_ref, e2_ref, state_ref, rows, a_row, mask, tri, left, *, reverse, head_off):
    dt = dt_ref[0, rows, :]
    cs2 = _tri_cumsum(tri, dt * a_row) * LOG2E
    tot2 = cs2[0:1, :] if reverse else cs2[CHUNK - 1:CHUNK, :]
    src_t = (jnp.log(dt) * LOG2E - cs2).T
    ecs = jnp.exp2(cs2)
    dtw = dt * jnp.exp2(tot2 - cs2)
    etot = jnp.broadcast_to(jnp.exp2(tot2), (SUBLANES, LANES))
    wide = _dot(jnp.concatenate([_split2(ecs), _split2(dtw), _split2(etot)], axis=0), e2_ref[...])
    ecs_w = wide[0:CHUNK]
    dtw_w = wide[CHUNK:2 * CHUNK].astype(BF16)
    etot_w = wide[2 * CHUNK:2 * CHUNK + 1]

    y_groups = []
    for g in range(GROUPS):
        gcols = slice(g * GROUP_COLS, (g + 1) * GROUP_COLS)
        b_g = xc_ref[0, rows, D_INNER + g * NSTATE:D_INNER + (g + 1) * NSTATE]
        c_off = D_INNER + GROUPS * NSTATE + g * NSTATE
        c_g = xc_ref[0, rows, c_off:c_off + NSTATE]
        cb = lax.dot_general(c_g, b_g, (((1,), (1,)), ((), ())), preferred_element_type=F32)
        s_old = state_ref[g]
        y_off = _dot(c_g, s_old.astype(BF16)) * ecs_w[:, gcols]
        y_pairs = []
        for p in range(HEADS_PER_GROUP // 2):
            h0 = g * HEADS_PER_GROUP + 2 * p
            x_pair = xc_ref[0, rows, h0 * HEADDIM:(h0 + 2) * HEADDIM]
            mats = []
            for cc in (head_off + h0, head_off + h0 + 1):
                arg = cs2[:, cc:cc + 1] + src_t[cc:cc + 1, :]
                mats.append((cb * jnp.exp2(jnp.where(mask, arg, -jnp.inf))).astype(BF16))
            zero = jnp.zeros_like(x_pair)
            rhs = jnp.concatenate([jnp.where(left, x_pair, zero), jnp.where(left, zero, x_pair)], axis=0)
            y_pairs.append(_dot(jnp.concatenate(mats, axis=1), rhs))
        y_groups.append(jnp.concatenate(y_pairs, axis=1) + y_off)
        xw = xc_ref[0, rows, gcols] * dtw_w[:, gcols]
        s_new = lax.dot_general(b_g, xw, (((0,), (0,)), ((), ())), preferred_element_type=F32)
        state_ref[g] = s_old * etot_w[:, gcols] + s_new
    return jnp.concatenate(y_groups, axis=1)


def _ssd_setup(alog_ref, state_ref, *, reverse):
    @pl.when(pl.program_id(1) == 0)
    def _():
        state_ref[...] = jnp.zeros_like(state_ref)

    row = lax.broadcasted_iota(jnp.int32, (CHUNK, CHUNK), 0)
    col = lax.broadcasted_iota(jnp.int32, (CHUNK, CHUNK), 1)
    mask = (col >= row) if reverse else (row >= col)
    left = lax.broadcasted_iota(jnp.int32, (CHUNK, LANES), 1) < HEADDIM
    return -jnp.exp(alog_ref[...]), mask, mask.astype(BF16), left


def _ssd_fwd_kernel(xc_ref, dt_ref, alog_ref, e2_ref, dskip_ref, y_ref, state_ref, *, nchunks):
    a_row, mask, tri, left = _ssd_setup(alog_ref, state_ref, reverse=False)

    def body(c, carry):
        rows = pl.ds(pl.multiple_of(c * CHUNK, CHUNK), CHUNK)
        y = _ssd_chunk(xc_ref, dt_ref, e2_ref, state_ref, rows, a_row, mask, tri, left,
                       reverse=False, head_off=0)
        y = y + dskip_ref[...] * xc_ref[0, rows, 0:D_INNER].astype(F32)
        y_ref[0, rows, :] = y.astype(y_ref.dtype)
        return carry

    lax.fori_loop(0, nchunks, body, 0)


def _ssd_bwd_kernel(xc_ref, dt_ref, alog_ref, e2_ref, yf_ref, zs_ref, nw_ref, o_ref, state_ref, *, nchunks):
    a_row, mask, tri, left = _ssd_setup(alog_ref, state_ref, reverse=True)

    def body(i, carry):
        rows = pl.ds(pl.multiple_of((nchunks - 1 - i) * CHUNK, CHUNK), CHUNK)
        y = _ssd_chunk(xc_ref, dt_ref, e2_ref, state_ref, rows, a_row, mask, tri, left,
                       reverse=True, head_off=HEADS)
        y = y + yf_ref[0, rows, :].astype(F32)
        yg = y * zs_ref[0, rows, :].astype(F32)
        yn = yg * lax.rsqrt(jnp.mean(yg * yg, axis=-1, keepdims=True) + EPS) * nw_ref[...]
        o_ref[0, rows, :] = yn.astype(o_ref.dtype)
        return carry

    lax.fori_loop(0, nchunks, body, 0)


def _expansion(head_off):
    r = lax.broadcasted_iota(jnp.int32, (LANES, D_INNER), 0)
    c = lax.broadcasted_iota(jnp.int32, (LANES, D_INNER), 1)
    e = (r == head_off + c // HEADDIM).astype(BF16)
    return jnp.concatenate([e, e], axis=0)


def _ssd(xc, dt, zs, a_log, d_skip, norm_w, *, tb=1024):
    bsz, seq, _ = xc.shape
    nb = seq // tb
    nchunks = tb // CHUNK
    pad = jnp.zeros((1, LANES - 2 * HEADS), F32)
    alog_row = jnp.concatenate([a_log.reshape(1, 2 * HEADS), pad], axis=1)
    dskip_row = jnp.repeat(d_skip, HEADDIM).reshape(1, D_INNER)

    def specs(pos):
        return [
            pl.BlockSpec((1, tb, CONV_DIM), lambda b, j: (b, pos(j), 0)),
            pl.BlockSpec((1, tb, LANES), lambda b, j: (b, pos(j), 0)),
            _const_spec((1, LANES)),
            _const_spec((2 * LANES, D_INNER)),
        ]

    scratch = [pltpu.VMEM((GROUPS, NSTATE, GROUP_COLS), F32)]
    params = pltpu.CompilerParams(
        dimension_semantics=("arbitrary", "arbitrary"), vmem_limit_bytes=VMEM_LIMIT)

    y_fw = pl.pallas_call(
        functools.partial(_ssd_fwd_kernel, nchunks=nchunks),
        grid=(bsz, nb),
        in_specs=specs(lambda j: j) + [_const_spec((1, D_INNER))],
        out_specs=pl.BlockSpec((1, tb, D_INNER), lambda b, j: (b, j, 0)),
        out_shape=jax.ShapeDtypeStruct((bsz, seq, D_INNER), BF16),
        scratch_shapes=scratch,
        compiler_params=params,
        name="ssd_forward",
    )(xc, dt, alog_row, _expansion(0), dskip_row)

    bwd_pos = lambda j: nb - 1 - j
    tok = lambda width: pl.BlockSpec((1, tb, width), lambda b, j: (b, bwd_pos(j), 0))
    return pl.pallas_call(
        functools.partial(_ssd_bwd_kernel, nchunks=nchunks),
        grid=(bsz, nb),
        in_specs=specs(bwd_pos) + [tok(D_INNER), tok(D_INNER), _const_spec((1, D_INNER))],
        out_specs=tok(D_INNER),
        out_shape=jax.ShapeDtypeStruct((bsz, seq, D_INNER), BF16),
        scratch_shapes=scratch,
        compiler_params=params,
        name="ssd_backward",
    )(xc, dt, alog_row, _expansion(HEADS), y_fw, zs, norm_w.reshape(1, D_INNER))


def _merge_kernel(h_ref, mod_ref, yn_ref, vm_ref, vp_ref, vn_ref, gates_ref, cw_ref, cb_ref, lng_ref, lnb_ref,
                  wpa_ref, wpb_ref, wo_ref, o_ref, sh_ref, vc_ref, *, tm):
    i = pl.program_id(1)
    ni = pl.num_programs(1)
    ext = tm + 2 * HALO
    zero = jnp.zeros((HALO, D_MODEL), F32)
    sh_ref[0, 0:HALO, :] = jnp.where(i == 0, zero, vp_ref[0].astype(F32))
    sh_ref[0, HALO:HALO + tm, :] = vm_ref[0].astype(F32)
    sh_ref[0, HALO + tm:ext, :] = jnp.where(i == ni - 1, zero, vn_ref[0].astype(F32))
    for c in range(D_MODEL // CONV_COLS):
        cols = slice(c * CONV_COLS, (c + 1) * CONV_COLS)
        n_tiles = ext // SUBLANES
        tiles = [sh_ref[0, t * SUBLANES:(t + 1) * SUBLANES, cols] for t in range(n_tiles)]
        shifted = _row_windows(tiles, n_tiles - 1, list(range(1, SUBLANES)))
        for r in range(1, SUBLANES):
            sh_ref[r, 0:ext - SUBLANES, cols] = shifted[r]
        for rb in range(tm // CONV_ROWS):
            acc = jnp.broadcast_to(cb_ref[:, cols], (CONV_ROWS, CONV_COLS))
            for k in range(CONF_K):
                off = HALO - CONF_K // 2 + k
                start = rb * CONV_ROWS + (off // SUBLANES) * SUBLANES
                acc = acc + cw_ref[k:k + 1, cols] * sh_ref[off % SUBLANES, start:start + CONV_ROWS, cols]
            vc_ref[rb * CONV_ROWS:(rb + 1) * CONV_ROWS, cols] = acc
    vc = vc_ref[...]
    mu = jnp.mean(vc, axis=-1, keepdims=True)
    cen = vc - mu
    var = jnp.mean(cen * cen, axis=-1, keepdims=True)
    vb = _silu(cen * lax.rsqrt(var + EPS) * lng_ref[...] + lnb_ref[...]).astype(BF16)
    br_b = _dot(vb, wpb_ref[...])
    br_a = _dot(yn_ref[0], wpa_ref[...])
    g_a = gates_ref[0, :, 0:D_MODEL].astype(F32)
    g_b = gates_ref[0, :, D_MODEL:2 * D_MODEL].astype(F32)
    m = (g_a * br_a + g_b * br_b).astype(BF16)
    o_ref[0] = h_ref[0] + mod_ref[0, 5:6, :] * _dot(m, wo_ref[...])


def _merge(h, mods, yn, v, gates, conv_w, conv_b, ln_g, ln_b, w_pa, w_pb, w_out, *, tm=256):
    bsz, seq, _ = h.shape
    hb = tm // HALO
    nhb = seq // HALO
    cw = jnp.concatenate([conv_w, jnp.zeros((32 - CONF_K, D_MODEL), F32)], axis=0)

    def tok(width):
        return pl.BlockSpec((1, tm, width), lambda b, i: (b, i, 0))

    row = lambda a: a.reshape(1, D_MODEL)
    return pl.pallas_call(
        functools.partial(_merge_kernel, tm=tm),
        grid=(bsz, seq // tm),
        in_specs=[
            tok(D_MODEL),
            pl.BlockSpec((1, N_MOD, D_MODEL), lambda b, i: (b, 0, 0)),
            tok(D_INNER),
            tok(D_MODEL),
            pl.BlockSpec((1, HALO, D_MODEL), lambda b, i: (b, jnp.maximum(i * hb - 1, 0), 0)),
            pl.BlockSpec((1, HALO, D_MODEL), lambda b, i: (b, jnp.minimum((i + 1) * hb, nhb - 1), 0)),
            tok(2 * D_MODEL),
            _const_spec((32, D_MODEL)),
            _const_spec((1, D_MODEL)),
            _const_spec((1, D_MODEL)),
            _const_spec((1, D_MODEL)),
            _const_spec((D_INNER, D_MODEL)),
            _const_spec((D_MODEL, D_MODEL)),
            _const_spec((D_MODEL, D_MODEL)),
        ],
        out_specs=tok(D_MODEL),
        out_shape=jax.ShapeDtypeStruct(h.shape, F32),
        scratch_shapes=[pltpu.VMEM((SUBLANES, tm + 2 * HALO, D_MODEL), F32), pltpu.VMEM((tm, D_MODEL), F32)],
        compiler_params=pltpu.CompilerParams(
            dimension_semantics=("parallel", "parallel"), vmem_limit_bytes=VMEM_LIMIT),
        name="mixer_merge",
    )(h, mods, yn, v, v, v, gates, cw, row(conv_b), row(ln_g), row(ln_b),
      w_pa.astype(BF16), w_pb.astype(BF16), w_out.astype(BF16))


def _trunk(x, mods_all, w):
    depth = w["w_in"].shape[0]
    h = x
    for l in range(depth):
        mods = mods_all[l]
        h = _ffn(h, mods, w["ffn1_norm"][l], w["ffn1_up"][l], w["ffn1_down"][l], w["final_norm"],
                 mod_base=0, final=False)
        zs, xc, v, gates, dt = _inproj(h, mods, w["mix_norm"][l], w["w_in"][l], w["dt_bias"][l],
                                       w["ssm_conv_w"][l], w["ssm_conv_b"][l])
        yn = _ssd(xc, dt, zs, w["a_log"][l], w["d_skip"][l], w["ssm_norm"][l])
        h = _merge(h, mods, yn, v, gates, w["conf_conv_w"][l], w["conf_conv_b"][l], w["conf_ln_g"][l],
                   w["conf_ln_b"][l], w["w_proj_ssd"][l], w["w_proj_conv"][l], w["w_out"][l])
        h = _ffn(h, mods, w["ffn2_norm"][l], w["ffn2_up"][l], w["ffn2_down"][l], w["final_norm"],
                 mod_base=6, final=(l == depth - 1))
    return h


def kernel(x_prompt, x_sample, c_prompt, c_sample, w_ada, b_ada, ffn1_norm, ffn1_up, ffn1_down, mix_norm, w_in,
           ssm_conv_w, ssm_conv_b, dt_bias, a_log, d_skip, ssm_norm, w_proj_ssd, conf_conv_w, conf_conv_b,
           conf_ln_g, conf_ln_b, w_proj_conv, w_out, ffn2_norm, ffn2_up, ffn2_down, final_norm):
    w = dict(ffn1_norm=ffn1_norm, ffn1_up=ffn1_up, ffn1_down=ffn1_down, mix_norm=mix_norm, w_in=w_in,
             ssm_conv_w=ssm_conv_w, ssm_conv_b=ssm_conv_b, dt_bias=dt_bias, a_log=a_log, d_skip=d_skip,
             ssm_norm=ssm_norm, w_proj_ssd=w_proj_ssd, conf_conv_w=conf_conv_w, conf_conv_b=conf_conv_b,
             conf_ln_g=conf_ln_g, conf_ln_b=conf_ln_b, w_proj_conv=w_proj_conv, w_out=w_out,
             ffn2_norm=ffn2_norm, ffn2_up=ffn2_up, ffn2_down=ffn2_down, final_norm=final_norm)
    nb = x_prompt.shape[0]
    depth = w_ada.shape[0]
    c_all = jnp.concatenate([c_prompt, c_sample], axis=0)
    mods = _mods(c_all, w_ada, b_ada).reshape(depth, c_all.shape[0], N_MOD, D_MODEL)
    y_prompt = _trunk(x_prompt, mods[:, :nb], w)
    y_sample = _trunk(x_sample, mods[:, nb:], w)
    return (y_prompt, y_sample)
```

```python
import functools
import math

import jax
import jax.numpy as jnp
from jax import lax
from jax.experimental import pallas as pl
from jax.experimental.pallas import tpu as pltpu

D_MODEL = 1024
D_INNER = 2 * D_MODEL
HEADDIM = 64
HEADS = D_INNER // HEADDIM
GROUPS = 4
HEADS_PER_GROUP = HEADS // GROUPS
GROUP_COLS = HEADS_PER_GROUP * HEADDIM
NSTATE = 128
CHUNK = 128
SSM_K = 5
CONV_DIM = D_INNER + 2 * GROUPS * NSTATE
CONF_K = 31
FFN_DIM = 2816
N_MOD = 9
EPS = 1e-6
OFF_XBC = D_INNER
OFF_DT = OFF_XBC + CONV_DIM
OFF_GLU = OFF_DT + 2 * HEADS
OFF_GATE = OFF_GLU + 2 * D_MODEL
IN_COLS = OFF_GATE + 2 * D_MODEL

LANES = 128
SUBLANES = 8
HALO = 16
FFN_COLS = 256
PROJ_COLS = 512
CHUNK_UNROLL = 4
CONV_ROWS = 64
CONV_COLS = 2 * LANES
VMEM_LIMIT = 56 * 1024 * 1024
LOG2E = math.log2(math.e)

F32 = jnp.float32
BF16 = jnp.bfloat16


def _dot(a, b):
    return jnp.dot(a, b, preferred_element_type=F32)


def _rms_mod(x, norm_w, shift, scale):
    y = x * lax.rsqrt(jnp.mean(x * x, axis=-1, keepdims=True) + EPS)
    return (y * norm_w) * (1.0 + scale) + shift


def _silu(x):
    return x * jax.nn.sigmoid(x)


def _row_windows(tiles, n_out, offsets):
    width = tiles[0].shape[1]
    sub = lax.broadcasted_iota(jnp.int32, (SUBLANES, width), 0)
    rolled = {}
    out = {}
    for o in offsets:
        q, r = divmod(o, SUBLANES)
        if r == 0:
            out[o] = jnp.concatenate(tiles[q:q + n_out], axis=0)
            continue
        if r not in rolled:
            rolled[r] = [pltpu.roll(t, SUBLANES - r, axis=0) for t in tiles]
        from_next = sub >= SUBLANES - r
        rt = rolled[r]
        out[o] = jnp.concatenate(
            [jnp.where(from_next, rt[q + i + 1], rt[q + i]) for i in range(n_out)], axis=0)
    return out


def _const_spec(shape):
    nd = len(shape)
    return pl.BlockSpec(shape, lambda *_: (0,) * nd, pipeline_mode=pl.Buffered(1))


def _mods_kernel(c_ref, w_ref, b_ref, o_ref):
    c_act = _silu(c_ref[...])
    o_ref[0] = jnp.dot(c_act, w_ref[0], preferred_element_type=F32,
                       precision=lax.Precision.HIGHEST) + b_ref[0]


def _mods(c_all, w_ada, b_ada):
    depth, _, cols = w_ada.shape
    rows = c_all.shape[0]
    tn = D_MODEL
    return pl.pallas_call(
        _mods_kernel,
        grid=(depth, cols // tn),
        in_specs=[
            pl.BlockSpec((rows, D_MODEL), lambda l, j: (0, 0)),
            pl.BlockSpec((1, D_MODEL, tn), lambda l, j: (l, 0, j)),
            pl.BlockSpec((1, 1, tn), lambda l, j: (l, 0, j)),
        ],
        out_specs=pl.BlockSpec((1, rows, tn), lambda l, j: (l, 0, j)),
        out_shape=jax.ShapeDtypeStruct((depth, rows, cols), F32),
        name="adaln_mods",
    )(c_all, w_ada, b_ada.reshape(depth, 1, cols))


def _ffn_kernel(h_ref, mod_ref, nw_ref, wup_ref, wdn_ref, fnw_ref, o_ref, act_ref, *, mod_base, final):
    x = h_ref[0]
    shift = mod_ref[0, mod_base:mod_base + 1, :]
    scale = mod_ref[0, mod_base + 1:mod_base + 2, :]
    gate = mod_ref[0, mod_base + 2:mod_base + 3, :]
    u = _rms_mod(x, nw_ref[...], shift, scale).astype(BF16)
    for j in range(FFN_DIM // FFN_COLS):
        cols = slice(j * FFN_COLS, (j + 1) * FFN_COLS)
        gcols = slice(FFN_DIM + j * FFN_COLS, FFN_DIM + (j + 1) * FFN_COLS)
        a = _dot(u, wup_ref[:, cols])
        b = _dot(u, wup_ref[:, gcols])
        act_ref[:, cols] = (_silu(a) * b).astype(BF16)
    y = _dot(act_ref[...], wdn_ref[...])
    out = x + (0.5 * gate) * y
    if final:
        out = out * lax.rsqrt(jnp.mean(out * out, axis=-1, keepdims=True) + EPS) * fnw_ref[...]
    o_ref[0] = out


def _ffn(h, mods, norm_w, w_up, w_down, final_w, *, mod_base, final, tm=512):
    bsz, seq, _ = h.shape
    kern = functools.partial(_ffn_kernel, mod_base=mod_base, final=final)
    return pl.pallas_call(
        kern,
        grid=(bsz, seq // tm),
        in_specs=[
            pl.BlockSpec((1, tm, D_MODEL), lambda b, i: (b, i, 0)),
            pl.BlockSpec((1, N_MOD, D_MODEL), lambda b, i: (b, 0, 0)),
            _const_spec((1, D_MODEL)),
            _const_spec((D_MODEL, 2 * FFN_DIM)),
            _const_spec((FFN_DIM, D_MODEL)),
            _const_spec((1, D_MODEL)),
        ],
        out_specs=pl.BlockSpec((1, tm, D_MODEL), lambda b, i: (b, i, 0)),
        out_shape=jax.ShapeDtypeStruct(h.shape, F32),
        scratch_shapes=[pltpu.VMEM((tm, FFN_DIM), BF16)],
        compiler_params=pltpu.CompilerParams(
            dimension_semantics=("parallel", "parallel"), vmem_limit_bytes=VMEM_LIMIT),
        name="ffn_final" if final else "ffn",
    )(h, mods, norm_w.reshape(1, D_MODEL), w_up.astype(BF16), w_down.astype(BF16),
      final_w.reshape(1, D_MODEL))


PW_Z = 0
PW_XBC = PW_Z + D_INNER
PW_GLU = PW_XBC + CONV_DIM
PW_GATE = PW_GLU + 2 * D_MODEL
PW_DT = PW_GATE + 2 * D_MODEL
PW_COLS = PW_DT + LANES


def _inproj_kernel(h_ref, hp_ref, hn_ref, mod_ref, nw_ref, w_ref, dtb_ref, cw_ref, cb_ref,
                   zs_ref, xc_ref, v_ref, gates_ref, dt_ref, u_ref, p_ref, *, tm):
    i = pl.program_id(1)
    first = i == 0
    last = i == pl.num_programs(1) - 1
    hx = jnp.concatenate([hp_ref[0], h_ref[0], hn_ref[0]], axis=0)
    u_ref[...] = _rms_mod(hx, nw_ref[...], mod_ref[0, 3:4, :], mod_ref[0, 4:5, :]).astype(BF16)
    zero = jnp.zeros((HALO, PROJ_COLS), F32)
    u_ext = slice(0, tm + 2 * HALO)
    u = slice(HALO, HALO + tm)

    def proj(rows, base, j):
        return _dot(u_ref[rows, :], w_ref[:, base + j * PROJ_COLS:base + (j + 1) * PROJ_COLS])

    def z_unit(j):
        zs_ref[0, :, j * PROJ_COLS:(j + 1) * PROJ_COLS] = _silu(proj(u, PW_Z, j)).astype(BF16)

    def conv_unit(j):
        p = proj(u_ext, PW_XBC, j)
        p_ref[0:HALO, :] = jnp.where(first, zero, p[0:HALO])
        p_ref[HALO:HALO + tm, :] = p[HALO:HALO + tm]
        p_ref[HALO + tm:, :] = jnp.where(last, zero, p[HALO + tm:])
        for c in range(PROJ_COLS // CONV_COLS):
            pc = slice(c * CONV_COLS, (c + 1) * CONV_COLS)
            cols = slice(j * PROJ_COLS + c * CONV_COLS, j * PROJ_COLS + (c + 1) * CONV_COLS)
            for rb in range(tm // CONV_ROWS):
                acc = jnp.broadcast_to(cb_ref[:, cols], (CONV_ROWS, CONV_COLS))
                for k in range(SSM_K):
                    start = HALO - SSM_K // 2 + k + rb * CONV_ROWS
                    acc = acc + cw_ref[k:k + 1, cols] * p_ref[start:start + CONV_ROWS, pc]
                xc_ref[0, rb * CONV_ROWS:(rb + 1) * CONV_ROWS, cols] = _silu(acc).astype(BF16)

    def glu_unit(j):
        va = proj(u, PW_GLU, j)
        vg = proj(u, PW_GLU + D_MODEL, j)
        v_ref[0, :, j * PROJ_COLS:(j + 1) * PROJ_COLS] = (va * jax.nn.sigmoid(vg)).astype(BF16)

    def gate_unit(j):
        gates_ref[0, :, j * PROJ_COLS:(j + 1) * PROJ_COLS] = jax.nn.sigmoid(proj(u, PW_GATE, j)).astype(BF16)

    light = ([functools.partial(z_unit, j) for j in range(D_INNER // PROJ_COLS)]
             + [functools.partial(glu_unit, j) for j in range(D_MODEL // PROJ_COLS)]
             + [functools.partial(gate_unit, j) for j in range(2 * D_MODEL // PROJ_COLS)])
    n_conv = CONV_DIM // PROJ_COLS
    for j in range(n_conv):
        conv_unit(j)
        for unit in light[j * len(light) // n_conv:(j + 1) * len(light) // n_conv]:
            unit()
    dtr = _dot(u_ref[HALO:HALO + tm, :], w_ref[:, PW_DT:PW_DT + LANES]) + dtb_ref[...]
    dt_ref[0] = jnp.maximum(dtr, 0.0) + jnp.log(1.0 + jnp.exp(-jnp.abs(dtr)))


def _inproj(h, mods, norm_w, w_in, dt_bias, conv_w, conv_b, *, tm=512):
    bsz, seq, _ = h.shape
    hb = tm // HALO
    nhb = seq // HALO
    w = jnp.concatenate([
        w_in[:, :OFF_DT], w_in[:, OFF_GLU:], w_in[:, OFF_DT:OFF_GLU],
        jnp.zeros((D_MODEL, LANES - 2 * HEADS), w_in.dtype)], axis=1).astype(BF16)
    dtb = jnp.concatenate([dt_bias.reshape(1, 2 * HEADS), jnp.zeros((1, LANES - 2 * HEADS), F32)], axis=1)
    cw = jnp.concatenate([conv_w, jnp.zeros((SUBLANES - SSM_K, CONV_DIM), F32)], axis=0)

    def tok(width):
        return pl.BlockSpec((1, tm, width), lambda b, i: (b, i, 0))

    return pl.pallas_call(
        functools.partial(_inproj_kernel, tm=tm),
        grid=(bsz, seq // tm),
        in_specs=[
            tok(D_MODEL),
            pl.BlockSpec((1, HALO, D_MODEL), lambda b, i: (b, jnp.maximum(i * hb - 1, 0), 0)),
            pl.BlockSpec((1, HALO, D_MODEL), lambda b, i: (b, jnp.minimum((i + 1) * hb, nhb - 1), 0)),
            pl.BlockSpec((1, N_MOD, D_MODEL), lambda b, i: (b, 0, 0)),
            _const_spec((1, D_MODEL)),
            _const_spec((D_MODEL, PW_COLS)),
            _const_spec((1, LANES)),
            _const_spec((SUBLANES, CONV_DIM)),
            _const_spec((1, CONV_DIM)),
        ],
        out_specs=[tok(D_INNER), tok(CONV_DIM), tok(D_MODEL), tok(2 * D_MODEL), tok(LANES)],
        out_shape=[
            jax.ShapeDtypeStruct((bsz, seq, D_INNER), BF16),
            jax.ShapeDtypeStruct((bsz, seq, CONV_DIM), BF16),
            jax.ShapeDtypeStruct((bsz, seq, D_MODEL), BF16),
            jax.ShapeDtypeStruct((bsz, seq, 2 * D_MODEL), BF16),
            jax.ShapeDtypeStruct((bsz, seq, LANES), F32),
        ],
        scratch_shapes=[pltpu.VMEM((tm + 2 * HALO, D_MODEL), BF16),
                        pltpu.VMEM((tm + 2 * HALO, PROJ_COLS), F32)],
        compiler_params=pltpu.CompilerParams(
            dimension_semantics=("parallel", "parallel"), vmem_limit_bytes=VMEM_LIMIT),
        name="mixer_inproj",
    )(h, h, h, mods, norm_w.reshape(1, D_MODEL), w, dtb, cw, conv_b.reshape(1, CONV_DIM))


def _split2(v):
    hi = v.astype(BF16)
    lo = (v - hi.astype(F32)).astype(BF16)
    return jnp.concatenate([hi, lo], axis=1)


def _tri_cumsum(tri, v):
    p1 = v.astype(BF16)
    r1 = v - p1.astype(F32)
    p2 = r1.astype(BF16)
    p3 = (r1 - p2.astype(F32)).astype(BF16)
    return _dot(tri, p1) + _dot(tri, p2) + _dot(tri, p3)


def _ssd_chunk(xc_ref, dt_ref, e2_ref, state_ref, rows, a_row, mask, tri, left, *, reverse, head_off):
    dt = dt_ref[0, rows, :]
    cs2 = _tri_cumsum(tri, dt * a_row) * LOG2E
    tot2 = cs2[0:1, :] if reverse else cs2[CHUNK - 1:CHUNK, :]
    src_t = (jnp.log(dt) * LOG2E - cs2).T
    ecs = jnp.exp2(cs2)
    dtw = dt * jnp.exp2(tot2 - cs2)
    etot = jnp.broadcast_to(jnp.exp2(tot2), (SUBLANES, LANES))
    wide = _dot(jnp.concatenate([_split2(ecs), _split2(dtw), _split2(etot)], axis=0), e2_ref[...])
    ecs_w = wide[0:CHUNK]
    dtw_w = wide[CHUNK:2 * CHUNK].astype(BF16)
    etot_w = wide[2 * CHUNK:2 * CHUNK + 1]

    y_groups = []
    for g in range(GROUPS):
        gcols = slice(g * GROUP_COLS, (g + 1) * GROUP_COLS)
        b_g = xc_ref[0, rows, D_INNER + g * NSTATE:D_INNER + (g + 1) * NSTATE]
        c_off = D_INNER + GROUPS * NSTATE + g * NSTATE
        c_g = xc_ref[0, rows, c_off:c_off + NSTATE]
        cb = lax.dot_general(c_g, b_g, (((1,), (1,)), ((), ())), preferred_element_type=F32)
        s_old = state_ref[g]
        y_off = _dot(c_g, s_old.astype(BF16)) * ecs_w[:, gcols]
        y_pairs = []
        for p in range(HEADS_PER_GROUP // 2):
            h0 = g * HEADS_PER_GROUP + 2 * p
            x_pair = xc_ref[0, rows, h0 * HEADDIM:(h0 + 2) * HEADDIM]
            mats = []
            for cc in (head_off + h0, head_off + h0 + 1):
                arg = cs2[:, cc:cc + 1] + src_t[cc:cc + 1, :]
                mats.append((cb * jnp.exp2(jnp.where(mask, arg, -jnp.inf))).astype(BF16))
            zero = jnp.zeros_like(x_pair)
            rhs = jnp.concatenate([jnp.where(left, x_pair, zero), jnp.where(left, zero, x_pair)], axis=0)
            y_pairs.append(_dot(jnp.concatenate(mats, axis=1), rhs))
        y_groups.append(jnp.concatenate(y_pairs, axis=1) + y_off)
        xw = xc_ref[0, rows, gcols] * dtw_w[:, gcols]
        s_new = lax.dot_general(b_g, xw, (((0,), (0,)), ((), ())), preferred_element_type=F32)
        state_ref[g] = s_old * etot_w[:, gcols] + s_new
    return jnp.concatenate(y_groups, axis=1)


def _ssd_setup(alog_ref, state_ref, *, reverse):
    @pl.when(pl.program_id(1) == 0)
    def _():
        state_ref[...] = jnp.zeros_like(state_ref)

    row = lax.broadcasted_iota(jnp.int32, (CHUNK, CHUNK), 0)
    col = lax.broadcasted_iota(jnp.int32, (CHUNK, CHUNK), 1)
    mask = (col >= row) if reverse else (row >= col)
    left = lax.broadcasted_iota(jnp.int32, (CHUNK, LANES), 1) < HEADDIM
    return -jnp.exp(alog_ref[...]), mask, mask.astype(BF16), left


def _ssd_fwd_kernel(xc_ref, dt_ref, alog_ref, e2_ref, dskip_ref, y_ref, state_ref, *, nchunks):
    a_row, mask, tri, left = _ssd_setup(alog_ref, state_ref, reverse=False)

    def body(c, carry):
        rows = pl.ds(pl.multiple_of(c * CHUNK, CHUNK), CHUNK)
        y = _ssd_chunk(xc_ref, dt_ref, e2_ref, state_ref, rows, a_row, mask, tri, left,
                       reverse=False, head_off=0)
        y = y + dskip_ref[...] * xc_ref[0, rows, 0:D_INNER].astype(F32)
        y_ref[0, rows, :] = y.astype(y_ref.dtype)
        return carry

    lax.fori_loop(0, nchunks, body, 0, unroll=CHUNK_UNROLL)


def _ssd_bwd_kernel(xc_ref, dt_ref, alog_ref, e2_ref, yf_ref, zs_ref, nw_ref, o_ref, state_ref, *, nchunks):
    a_row, mask, tri, left = _ssd_setup(alog_ref, state_ref, reverse=True)

    def body(i, carry):
        rows = pl.ds(pl.multiple_of((nchunks - 1 - i) * CHUNK, CHUNK), CHUNK)
        y = _ssd_chunk(xc_ref, dt_ref, e2_ref, state_ref, rows, a_row, mask, tri, left,
                       reverse=True, head_off=HEADS)
        y = y + yf_ref[0, rows, :].astype(F32)
        yg = y * zs_ref[0, rows, :].astype(F32)
        yn = yg * lax.rsqrt(jnp.mean(yg * yg, axis=-1, keepdims=True) + EPS) * nw_ref[...]
        o_ref[0, rows, :] = yn.astype(o_ref.dtype)
        return carry

    lax.fori_loop(0, nchunks, body, 0, unroll=CHUNK_UNROLL)


def _expansion(head_off):
    r = lax.broadcasted_iota(jnp.int32, (LANES, D_INNER), 0)
    c = lax.broadcasted_iota(jnp.int32, (LANES, D_INNER), 1)
    e = (r == head_off + c // HEADDIM).astype(BF16)
    return jnp.concatenate([e, e], axis=0)


def _ssd(xc, dt, zs, a_log, d_skip, norm_w, *, tb=1024):
    bsz, seq, _ = xc.shape
    nb = seq // tb
    nchunks = tb // CHUNK
    pad = jnp.zeros((1, LANES - 2 * HEADS), F32)
    alog_row = jnp.concatenate([a_log.reshape(1, 2 * HEADS), pad], axis=1)
    dskip_row = jnp.repeat(d_skip, HEADDIM).reshape(1, D_INNER)

    def specs(pos):
        return [
            pl.BlockSpec((1, tb, CONV_DIM), lambda b, j: (b, pos(j), 0)),
            pl.BlockSpec((1, tb, LANES), lambda b, j: (b, pos(j), 0)),
            _const_spec((1, LANES)),
            _const_spec((2 * LANES, D_INNER)),
        ]

    scratch = [pltpu.VMEM((GROUPS, NSTATE, GROUP_COLS), F32)]
    params = pltpu.CompilerParams(
        dimension_semantics=("arbitrary", "arbitrary"), vmem_limit_bytes=VMEM_LIMIT)

    y_fw = pl.pallas_call(
        functools.partial(_ssd_fwd_kernel, nchunks=nchunks),
        grid=(bsz, nb),
        in_specs=specs(lambda j: j) + [_const_spec((1, D_INNER))],
        out_specs=pl.BlockSpec((1, tb, D_INNER), lambda b, j: (b, j, 0)),
        out_shape=jax.ShapeDtypeStruct((bsz, seq, D_INNER), BF16),
        scratch_shapes=scratch,
        compiler_params=params,
        name="ssd_forward",
    )(xc, dt, alog_row, _expansion(0), dskip_row)

    bwd_pos = lambda j: nb - 1 - j
    tok = lambda width: pl.BlockSpec((1, tb, width), lambda b, j: (b, bwd_pos(j), 0))
    return pl.pallas_call(
        functools.partial(_ssd_bwd_kernel, nchunks=nchunks),
        grid=(bsz, nb),
        in_specs=specs(bwd_pos) + [tok(D_INNER), tok(D_INNER), _const_spec((1, D_INNER))],
        out_specs=tok(D_INNER),
        out_shape=jax.ShapeDtypeStruct((bsz, seq, D_INNER), BF16),
        scratch_shapes=scratch,
        compiler_params=params,
        name="ssd_backward",
    )(xc, dt, alog_row, _expansion(HEADS), y_fw, zs, norm_w.reshape(1, D_INNER))


def _merge_kernel(h_ref, mod_ref, yn_ref, vm_ref, vp_ref, vn_ref, gates_ref, cw_ref, cb_ref, lng_ref, lnb_ref,
                  wpa_ref, wpb_ref, wo_ref, o_ref, sh_ref, vc_ref, *, tm):
    i = pl.program_id(1)
    ni = pl.num_programs(1)
    ext = tm + 2 * HALO
    zero = jnp.zeros((HALO, D_MODEL), F32)
    sh_ref[0, 0:HALO, :] = jnp.where(i == 0, zero, vp_ref[0].astype(F32))
    sh_ref[0, HALO:HALO + tm, :] = vm_ref[0].astype(F32)
    sh_ref[0, HALO + tm:ext, :] = jnp.where(i == ni - 1, zero, vn_ref[0].astype(F32))
    for c in range(D_MODEL // CONV_COLS):
        cols = slice(c * CONV_COLS, (c + 1) * CONV_COLS)
        n_tiles = ext // SUBLANES
        tiles = [sh_ref[0, t * SUBLANES:(t + 1) * SUBLANES, cols] for t in range(n_tiles)]
        shifted = _row_windows(tiles, n_tiles - 1, list(range(1, SUBLANES)))
        for r in range(1, SUBLANES):
            sh_ref[r, 0:ext - SUBLANES, cols] = shifted[r]
        for rb in range(tm // CONV_ROWS):
            acc = jnp.broadcast_to(cb_ref[:, cols], (CONV_ROWS, CONV_COLS))
            for k in range(CONF_K):
                off = HALO - CONF_K // 2 + k
                start = rb * CONV_ROWS + (off // SUBLANES) * SUBLANES
                acc = acc + cw_ref[k:k + 1, cols] * sh_ref[off % SUBLANES, start:start + CONV_ROWS, cols]
            vc_ref[rb * CONV_ROWS:(rb + 1) * CONV_ROWS, cols] = acc
    vc = vc_ref[...]
    mu = jnp.mean(vc, axis=-1, keepdims=True)
    cen = vc - mu
    var = jnp.mean(cen * cen, axis=-1, keepdims=True)
    vb = _silu(cen * lax.rsqrt(var + EPS) * lng_ref[...] + lnb_ref[...]).astype(BF16)
    br_b = _dot(vb, wpb_ref[...])
    br_a = _dot(yn_ref[0], wpa_ref[...])
    g_a = gates_ref[0, :, 0:D_MODEL].astype(F32)
    g_b = gates_ref[0, :, D_MODEL:2 * D_MODEL].astype(F32)
    m = (g_a * br_a + g_b * br_b).astype(BF16)
    o_ref[0] = h_ref[0] + mod_ref[0, 5:6, :] * _dot(m, wo_ref[...])


def _merge(h, mods, yn, v, gates, conv_w, conv_b, ln_g, ln_b, w_pa, w_pb, w_out, *, tm=256):
    bsz, seq, _ = h.shape
    hb = tm // HALO
    nhb = seq // HALO
    cw = jnp.concatenate([conv_w, jnp.zeros((32 - CONF_K, D_MODEL), F32)], axis=0)

    def tok(width):
        return pl.BlockSpec((1, tm, width), lambda b, i: (b, i, 0))

    row = lambda a: a.reshape(1, D_MODEL)
    return pl.pallas_call(
        functools.partial(_merge_kernel, tm=tm),
        grid=(bsz, seq // tm),
        in_specs=[
            tok(D_MODEL),
            pl.BlockSpec((1, N_MOD, D_MODEL), lambda b, i: (b, 0, 0)),
            tok(D_INNER),
            tok(D_MODEL),
            pl.BlockSpec((1, HALO, D_MODEL), lambda b, i: (b, jnp.maximum(i * hb - 1, 0), 0)),
            pl.BlockSpec((1, HALO, D_MODEL), lambda b, i: (b, jnp.minimum((i + 1) * hb, nhb - 1), 0)),
            tok(2 * D_MODEL),
            _const_spec((32, D_MODEL)),
            _const_spec((1, D_MODEL)),
            _const_spec((1, D_MODEL)),
            _const_spec((1, D_MODEL)),
            _const_spec((D_INNER, D_MODEL)),
            _const_spec((D_MODEL, D_MODEL)),
            _const_spec((D_MODEL, D_MODEL)),
        ],
        out_specs=tok(D_MODEL),
        out_shape=jax.ShapeDtypeStruct(h.shape, F32),
        scratch_shapes=[pltpu.VMEM((SUBLANES, tm + 2 * HALO, D_MODEL), F32), pltpu.VMEM((tm, D_MODEL), F32)],
        compiler_params=pltpu.CompilerParams(
            dimension_semantics=("parallel", "parallel"), vmem_limit_bytes=VMEM_LIMIT),
        name="mixer_merge",
    )(h, mods, yn, v, v, v, gates, cw, row(conv_b), row(ln_g), row(ln_b),
      w_pa.astype(BF16), w_pb.astype(BF16), w_out.astype(BF16))


def _trunk(x, mods_all, w):
    depth = w["w_in"].shape[0]
    h = x
    for l in range(depth):
        mods = mods_all[l]
        h = _ffn(h, mods, w["ffn1_norm"][l], w["ffn1_up"][l], w["ffn1_down"][l], w["final_norm"],
                 mod_base=0, final=False)
        zs, xc, v, gates, dt = _inproj(h, mods, w["mix_norm"][l], w["w_in"][l], w["dt_bias"][l],
                                       w["ssm_conv_w"][l], w["ssm_conv_b"][l])
        yn = _ssd(xc, dt, zs, w["a_log"][l], w["d_skip"][l], w["ssm_norm"][l])
        h = _merge(h, mods, yn, v, gates, w["conf_conv_w"][l], w["conf_conv_b"][l], w["conf_ln_g"][l],
                   w["conf_ln_b"][l], w["w_proj_ssd"][l], w["w_proj_conv"][l], w["w_out"][l])
        h = _ffn(h, mods, w["ffn2_norm"][l], w["ffn2_up"][l], w["ffn2_down"][l], w["final_norm"],
                 mod_base=6, final=(l == depth - 1))
    return h


def kernel(x_prompt, x_sample, c_prompt, c_sample, w_ada, b_ada, ffn1_norm, ffn1_up, ffn1_down, mix_norm, w_in,
           ssm_conv_w, ssm_conv_b, dt_bias, a_log, d_skip, ssm_norm, w_proj_ssd, conf_conv_w, conf_conv_b,
           conf_ln_g, conf_ln_b, w_proj_conv, w_out, ffn2_norm, ffn2_up, ffn2_down, final_norm):
    w = dict(ffn1_norm=ffn1_norm, ffn1_up=ffn1_up, ffn1_down=ffn1_down, mix_norm=mix_norm, w_in=w_in,
             ssm_conv_w=ssm_conv_w, ssm_conv_b=ssm_conv_b, dt_bias=dt_bias, a_log=a_log, d_skip=d_skip,
             ssm_norm=ssm_norm, w_proj_ssd=w_proj_ssd, conf_conv_w=conf_conv_w, conf_conv_b=conf_conv_b,
             conf_ln_g=conf_ln_g, conf_ln_b=conf_ln_b, w_proj_conv=w_proj_conv, w_out=w_out,
             ffn2_norm=ffn2_norm, ffn2_up=ffn2_up, ffn2_down=ffn2_down, final_norm=final_norm)
    nb = x_prompt.shape[0]
    depth = w_ada.shape[0]
    c_all = jnp.concatenate([c_prompt, c_sample], axis=0)
    mods = _mods(c_all, w_ada, b_ada).reshape(depth, c_all.shape[0], N_MOD, D_MODEL)
    y_prompt = _trunk(x_prompt, mods[:, :nb], w)
    y_sample = _trunk(x_sample, mods[:, nb:], w)
    return (y_prompt, y_sample)
```

```python
import functools
import math

import jax
import jax.numpy as jnp
from jax import lax
from jax.experimental import pallas as pl
from jax.experimental.pallas import tpu as pltpu

D_MODEL = 1024
D_INNER = 2 * D_MODEL
HEADDIM = 64
HEADS = D_INNER // HEADDIM
GROUPS = 4
HEADS_PER_GROUP = HEADS // GROUPS
GROUP_COLS = HEADS_PER_GROUP * HEADDIM
NSTATE = 128
CHUNK = 128
SSM_K = 5
CONV_DIM = D_INNER + 2 * GROUPS * NSTATE
CONF_K = 31
FFN_DIM = 2816
N_MOD = 9
EPS = 1e-6
OFF_XBC = D_INNER
OFF_DT = OFF_XBC + CONV_DIM
OFF_GLU = OFF_DT + 2 * HEADS
OFF_GATE = OFF_GLU + 2 * D_MODEL
IN_COLS = OFF_GATE + 2 * D_MODEL

LANES = 128
SUBLANES = 8
HALO = 16
FFN_COLS = 256
PROJ_COLS = 512
CHUNK_UNROLL = 4
CONV_ROWS = 64
CONV_COLS = 2 * LANES
VMEM_LIMIT = 56 * 1024 * 1024
LOG2E = math.log2(math.e)

F32 = jnp.float32
BF16 = jnp.bfloat16


def _dot(a, b):
    return jnp.dot(a, b, preferred_element_type=F32)


def _rms_mod(x, norm_w, shift, scale):
    y = x * lax.rsqrt(jnp.mean(x * x, axis=-1, keepdims=True) + EPS)
    return (y * norm_w) * (1.0 + scale) + shift


def _silu(x):
    return x * jax.nn.sigmoid(x)


def _row_windows(tiles, n_out, offsets):
    width = tiles[0].shape[1]
    sub = lax.broadcasted_iota(jnp.int32, (SUBLANES, width), 0)
    rolled = {}
    out = {}
    for o in offsets:
        q, r = divmod(o, SUBLANES)
        if r == 0:
            out[o] = jnp.concatenate(tiles[q:q + n_out], axis=0)
            continue
        if r not in rolled:
            rolled[r] = [pltpu.roll(t, SUBLANES - r, axis=0) for t in tiles]
        from_next = sub >= SUBLANES - r
        rt = rolled[r]
        out[o] = jnp.concatenate(
            [jnp.where(from_next, rt[q + i + 1], rt[q + i]) for i in range(n_out)], axis=0)
    return out


def _const_spec(shape):
    nd = len(shape)
    return pl.BlockSpec(shape, lambda *_: (0,) * nd, pipeline_mode=pl.Buffered(1))


def _layer_spec(shape, layer):
    nd = len(shape)
    return pl.BlockSpec((1,) + tuple(shape), lambda *_: (layer,) + (0,) * nd, pipeline_mode=pl.Buffered(1))


def _mods_kernel(c_ref, w_ref, b_ref, o_ref):
    c_act = _silu(c_ref[...])
    o_ref[0] = jnp.dot(c_act, w_ref[0], preferred_element_type=F32,
                       precision=lax.Precision.HIGHEST) + b_ref[0]


def _mods(c_all, w_ada, b_ada):
    depth, _, cols = w_ada.shape
    rows = c_all.shape[0]
    tn = D_MODEL
    return pl.pallas_call(
        _mods_kernel,
        grid=(depth, cols // tn),
        in_specs=[
            pl.BlockSpec((rows, D_MODEL), lambda l, j: (0, 0)),
            pl.BlockSpec((1, D_MODEL, tn), lambda l, j: (l, 0, j)),
            pl.BlockSpec((1, 1, tn), lambda l, j: (l, 0, j)),
        ],
        out_specs=pl.BlockSpec((1, rows, tn), lambda l, j: (l, 0, j)),
        out_shape=jax.ShapeDtypeStruct((depth, rows, cols), F32),
        name="adaln_mods",
    )(c_all, w_ada, b_ada.reshape(depth, 1, cols))


def _ffn_kernel(h_ref, mod_ref, nw_ref, wup_ref, wdn_ref, fnw_ref, o_ref, act_ref, *, mod_base, final):
    x = h_ref[0]
    shift = mod_ref[0, mod_base:mod_base + 1, :]
    scale = mod_ref[0, mod_base + 1:mod_base + 2, :]
    gate = mod_ref[0, mod_base + 2:mod_base + 3, :]
    u = _rms_mod(x, nw_ref[...], shift, scale).astype(BF16)
    for j in range(FFN_DIM // FFN_COLS):
        cols = slice(j * FFN_COLS, (j + 1) * FFN_COLS)
        gcols = slice(FFN_DIM + j * FFN_COLS, FFN_DIM + (j + 1) * FFN_COLS)
        a = _dot(u, wup_ref[0, :, cols])
        b = _dot(u, wup_ref[0, :, gcols])
        act_ref[:, cols] = (_silu(a) * b).astype(BF16)
    y = _dot(act_ref[...], wdn_ref[0])
    out = x + (0.5 * gate) * y
    if final:
        out = out * lax.rsqrt(jnp.mean(out * out, axis=-1, keepdims=True) + EPS) * fnw_ref[...]
    o_ref[0] = out


def _ffn(h, mods, norm_w, w_up, w_down, final_w, *, layer, mod_base, final, tm=512):
    bsz, seq, _ = h.shape
    kern = functools.partial(_ffn_kernel, mod_base=mod_base, final=final)
    return pl.pallas_call(
        kern,
        grid=(bsz, seq // tm),
        in_specs=[
            pl.BlockSpec((1, tm, D_MODEL), lambda b, i: (b, i, 0)),
            pl.BlockSpec((1, N_MOD, D_MODEL), lambda b, i: (b, 0, 0)),
            _const_spec((1, D_MODEL)),
            _layer_spec((D_MODEL, 2 * FFN_DIM), layer),
            _layer_spec((FFN_DIM, D_MODEL), layer),
            _const_spec((1, D_MODEL)),
        ],
        out_specs=pl.BlockSpec((1, tm, D_MODEL), lambda b, i: (b, i, 0)),
        out_shape=jax.ShapeDtypeStruct(h.shape, F32),
        scratch_shapes=[pltpu.VMEM((tm, FFN_DIM), BF16)],
        compiler_params=pltpu.CompilerParams(
            dimension_semantics=("parallel", "parallel"), vmem_limit_bytes=VMEM_LIMIT),
        name="ffn_final" if final else "ffn",
    )(h, mods, norm_w.reshape(1, D_MODEL), w_up, w_down, final_w.reshape(1, D_MODEL))


PW_Z = 0
PW_XBC = PW_Z + D_INNER
PW_GLU = PW_XBC + CONV_DIM
PW_GATE = PW_GLU + 2 * D_MODEL
PW_DT = PW_GATE + 2 * D_MODEL
PW_COLS = PW_DT + LANES


def _inproj_kernel(h_ref, hp_ref, hn_ref, mod_ref, nw_ref, w_ref, dtb_ref, cw_ref, cb_ref,
                   zs_ref, xc_ref, v_ref, gates_ref, dt_ref, u_ref, p_ref, *, tm):
    i = pl.program_id(1)
    first = i == 0
    last = i == pl.num_programs(1) - 1
    hx = jnp.concatenate([hp_ref[0], h_ref[0], hn_ref[0]], axis=0)
    u_ref[...] = _rms_mod(hx, nw_ref[...], mod_ref[0, 3:4, :], mod_ref[0, 4:5, :]).astype(BF16)
    zero = jnp.zeros((HALO, PROJ_COLS), F32)
    u_ext = slice(0, tm + 2 * HALO)
    u = slice(HALO, HALO + tm)

    def proj(rows, base, j):
        return _dot(u_ref[rows, :], w_ref[0, :, base + j * PROJ_COLS:base + (j + 1) * PROJ_COLS])

    def z_unit(j):
        zs_ref[0, :, j * PROJ_COLS:(j + 1) * PROJ_COLS] = _silu(proj(u, PW_Z, j)).astype(BF16)

    def conv_unit(j):
        p = proj(u_ext, PW_XBC, j)
        p_ref[0:HALO, :] = jnp.where(first, zero, p[0:HALO])
        p_ref[HALO:HALO + tm, :] = p[HALO:HALO + tm]
        p_ref[HALO + tm:, :] = jnp.where(last, zero, p[HALO + tm:])
        for c in range(PROJ_COLS // CONV_COLS):
            pc = slice(c * CONV_COLS, (c + 1) * CONV_COLS)
            cols = slice(j * PROJ_COLS + c * CONV_COLS, j * PROJ_COLS + (c + 1) * CONV_COLS)
            for rb in range(tm // CONV_ROWS):
                acc = jnp.broadcast_to(cb_ref[:, cols], (CONV_ROWS, CONV_COLS))
                for k in range(SSM_K):
                    start = HALO - SSM_K // 2 + k + rb * CONV_ROWS
                    acc = acc + cw_ref[k:k + 1, cols] * p_ref[start:start + CONV_ROWS, pc]
                xc_ref[0, rb * CONV_ROWS:(rb + 1) * CONV_ROWS, cols] = _silu(acc).astype(BF16)

    def glu_unit(j):
        va = proj(u, PW_GLU, j)
        vg = proj(u, PW_GLU + D_MODEL, j)
        v_ref[0, :, j * PROJ_COLS:(j + 1) * PROJ_COLS] = (va * jax.nn.sigmoid(vg)).astype(BF16)

    def gate_unit(j):
        gates_ref[0, :, j * PROJ_COLS:(j + 1) * PROJ_COLS] = jax.nn.sigmoid(proj(u, PW_GATE, j)).astype(BF16)

    light = ([functools.partial(z_unit, j) for j in range(D_INNER // PROJ_COLS)]
             + [functools.partial(glu_unit, j) for j in range(D_MODEL // PROJ_COLS)]
             + [functools.partial(gate_unit, j) for j in range(2 * D_MODEL // PROJ_COLS)])
    n_conv = CONV_DIM // PROJ_COLS
    for j in range(n_conv):
        conv_unit(j)
        for unit in light[j * len(light) // n_conv:(j + 1) * len(light) // n_conv]:
            unit()
    dtr = _dot(u_ref[HALO:HALO + tm, :], w_ref[0, :, PW_DT:PW_DT + LANES]) + dtb_ref[...]
    dt_ref[0] = jnp.maximum(dtr, 0.0) + jnp.log(1.0 + jnp.exp(-jnp.abs(dtr)))


def _inproj_weight(w_in):
    wb = w_in.astype(BF16)
    pad = jnp.zeros(wb.shape[:2] + (LANES - 2 * HEADS,), BF16)
    return jnp.concatenate([wb[..., :OFF_DT], wb[..., OFF_GLU:], wb[..., OFF_DT:OFF_GLU], pad], axis=-1)


def _inproj(h, mods, norm_w, w, dt_bias, conv_w, conv_b, *, layer, tm=512):
    bsz, seq, _ = h.shape
    hb = tm // HALO
    nhb = seq // HALO
    dtb = jnp.concatenate([dt_bias.reshape(1, 2 * HEADS), jnp.zeros((1, LANES - 2 * HEADS), F32)], axis=1)
    cw = jnp.concatenate([conv_w, jnp.zeros((SUBLANES - SSM_K, CONV_DIM), F32)], axis=0)

    def tok(width):
        return pl.BlockSpec((1, tm, width), lambda b, i: (b, i, 0))

    return pl.pallas_call(
        functools.partial(_inproj_kernel, tm=tm),
        grid=(bsz, seq // tm),
        in_specs=[
            tok(D_MODEL),
            pl.BlockSpec((1, HALO, D_MODEL), lambda b, i: (b, jnp.maximum(i * hb - 1, 0), 0)),
            pl.BlockSpec((1, HALO, D_MODEL), lambda b, i: (b, jnp.minimum((i + 1) * hb, nhb - 1), 0)),
            pl.BlockSpec((1, N_MOD, D_MODEL), lambda b, i: (b, 0, 0)),
            _const_spec((1, D_MODEL)),
            _layer_spec((D_MODEL, PW_COLS), layer),
            _const_spec((1, LANES)),
            _const_spec((SUBLANES, CONV_DIM)),
            _const_spec((1, CONV_DIM)),
        ],
        out_specs=[tok(D_INNER), tok(CONV_DIM), tok(D_MODEL), tok(2 * D_MODEL), tok(LANES)],
        out_shape=[
            jax.ShapeDtypeStruct((bsz, seq, D_INNER), BF16),
            jax.ShapeDtypeStruct((bsz, seq, CONV_DIM), BF16),
            jax.ShapeDtypeStruct((bsz, seq, D_MODEL), BF16),
            jax.ShapeDtypeStruct((bsz, seq, 2 * D_MODEL), BF16),
            jax.ShapeDtypeStruct((bsz, seq, LANES), F32),
        ],
        scratch_shapes=[pltpu.VMEM((tm + 2 * HALO, D_MODEL), BF16),
                        pltpu.VMEM((tm + 2 * HALO, PROJ_COLS), F32)],
        compiler_params=pltpu.CompilerParams(
            dimension_semantics=("parallel", "parallel"), vmem_limit_bytes=VMEM_LIMIT),
        name="mixer_inproj",
    )(h, h, h, mods, norm_w.reshape(1, D_MODEL), w, dtb, cw, conv_b.reshape(1, CONV_DIM))


def _split2(v):
    hi = v.astype(BF16)
    lo = (v - hi.astype(F32)).astype(BF16)
    return jnp.concatenate([hi, lo], axis=1)


def _tri_cumsum(tri, v):
    p1 = v.astype(BF16)
    r1 = v - p1.astype(F32)
    p2 = r1.astype(BF16)
    p3 = (r1 - p2.astype(F32)).astype(BF16)
    return _dot(tri, p1) + _dot(tri, p2) + _dot(tri, p3)


def _ssd_chunk(xc_ref, dt_ref, e2_ref, state_ref, rows, a_row, mask, tri, left, *, reverse, head_off):
    dt = dt_ref[0, rows, :]
    cs2 = _tri_cumsum(tri, dt * a_row) * LOG2E
    tot2 = cs2[0:1, :] if reverse else cs2[CHUNK - 1:CHUNK, :]
    src_t = (jnp.log(dt) * LOG2E - cs2).T
    ecs = jnp.exp2(cs2)
    dtw = dt * jnp.exp2(tot2 - cs2)
    etot = jnp.broadcast_to(jnp.exp2(tot2), (SUBLANES, LANES))
    wide = _dot(jnp.concatenate([_split2(ecs), _split2(dtw), _split2(etot)], axis=0), e2_ref[...])
    ecs_w = wide[0:CHUNK]
    dtw_w = wide[CHUNK:2 * CHUNK].astype(BF16)
    etot_w = wide[2 * CHUNK:2 * CHUNK + 1]

    y_groups = []
    for g in range(GROUPS):
        gcols = slice(g * GROUP_COLS, (g + 1) * GROUP_COLS)
        b_g = xc_ref[0, rows, D_INNER + g * NSTATE:D_INNER + (g + 1) * NSTATE]
        c_off = D_INNER + GROUPS * NSTATE + g * NSTATE
        c_g = xc_ref[0, rows, c_off:c_off + NSTATE]
        cb = lax.dot_general(c_g, b_g, (((1,), (1,)), ((), ())), preferred_element_type=F32)
        s_old = state_ref[g]
        y_off = _dot(c_g, s_old.astype(BF16)) * ecs_w[:, gcols]
        y_pairs = []
        for p in range(HEADS_PER_GROUP // 2):
            h0 = g * HEADS_PER_GROUP + 2 * p
            x_pair = xc_ref[0, rows, h0 * HEADDIM:(h0 + 2) * HEADDIM]
            mats = []
            for cc in (head_off + h0, head_off + h0 + 1):
                arg = cs2[:, cc:cc + 1] + src_t[cc:cc + 1, :]
                mats.append((cb * jnp.exp2(jnp.where(mask, arg, -jnp.inf))).astype(BF16))
            zero = jnp.zeros_like(x_pair)
            rhs = jnp.concatenate([jnp.where(left, x_pair, zero), jnp.where(left, zero, x_pair)], axis=0)
            y_pairs.append(_dot(jnp.concatenate(mats, axis=1), rhs))
        y_groups.append(jnp.concatenate(y_pairs, axis=1) + y_off)
        xw = xc_ref[0, rows, gcols] * dtw_w[:, gcols]
        s_new = lax.dot_general(b_g, xw, (((0,), (0,)), ((), ())), preferred_element_type=F32)
        state_ref[g] = s_old * etot_w[:, gcols] + s_new
    return jnp.concatenate(y_groups, axis=1)


def _ssd_setup(alog_ref, state_ref, *, reverse):
    @pl.when(pl.program_id(1) == 0)
    def _():
        state_ref[...] = jnp.zeros_like(state_ref)

    row = lax.broadcasted_iota(jnp.int32, (CHUNK, CHUNK), 0)
    col = lax.broadcasted_iota(jnp.int32, (CHUNK, CHUNK), 1)
    mask = (col >= row) if reverse else (row >= col)
    left = lax.broadcasted_iota(jnp.int32, (CHUNK, LANES), 1) < HEADDIM
    return -jnp.exp(alog_ref[...]), mask, mask.astype(BF16), left


def _ssd_fwd_kernel(xc_ref, dt_ref, alog_ref, e2_ref, dskip_ref, y_ref, state_ref, *, nchunks):
    a_row, mask, tri, left = _ssd_setup(alog_ref, state_ref, reverse=False)

    def body(c, carry):
        rows = pl.ds(pl.multiple_of(c * CHUNK, CHUNK), CHUNK)
        y = _ssd_chunk(xc_ref, dt_ref, e2_ref, state_ref, rows, a_row, mask, tri, left,
                       reverse=False, head_off=0)
        y = y + dskip_ref[...] * xc_ref[0, rows, 0:D_INNER].astype(F32)
        y_ref[0, rows, :] = y.astype(y_ref.dtype)
        return carry

    lax.fori_loop(0, nchunks, body, 0, unroll=CHUNK_UNROLL)


def _ssd_bwd_kernel(xc_ref, dt_ref, alog_ref, e2_ref, yf_ref, zs_ref, nw_ref, o_ref, state_ref, *, nchunks):
    a_row, mask, tri, left = _ssd_setup(alog_ref, state_ref, reverse=True)

    def body(i, carry):
        rows = pl.ds(pl.multiple_of((nchunks - 1 - i) * CHUNK, CHUNK), CHUNK)
        y = _ssd_chunk(xc_ref, dt_ref, e2_ref, state_ref, rows, a_row, mask, tri, left,
                       reverse=True, head_off=HEADS)
        y = y + yf_ref[0, rows, :].astype(F32)
        yg = y * zs_ref[0, rows, :].astype(F32)
        yn = yg * lax.rsqrt(jnp.mean(yg * yg, axis=-1, keepdims=True) + EPS) * nw_ref[...]
        o_ref[0, rows, :] = yn.astype(o_ref.dtype)
        return carry

    lax.fori_loop(0, nchunks, body, 0, unroll=CHUNK_UNROLL)


def _expansion(head_off):
    r = lax.broadcasted_iota(jnp.int32, (LANES, D_INNER), 0)
    c = lax.broadcasted_iota(jnp.int32, (LANES, D_INNER), 1)
    e = (r == head_off + c // HEADDIM).astype(BF16)
    return jnp.concatenate([e, e], axis=0)


def _ssd(xc, dt, zs, a_log, d_skip, norm_w, *, tb=1024):
    bsz, seq, _ = xc.shape
    nb = seq // tb
    nchunks = tb // CHUNK
    pad = jnp.zeros((1, LANES - 2 * HEADS), F32)
    alog_row = jnp.concatenate([a_log.reshape(1, 2 * HEADS), pad], axis=1)
    dskip_row = jnp.repeat(d_skip, HEADDIM).reshape(1, D_INNER)

    def specs(pos):
        return [
            pl.BlockSpec((1, tb, CONV_DIM), lambda b, j: (b, pos(j), 0)),
            pl.BlockSpec((1, tb, LANES), lambda b, j: (b, pos(j), 0)),
            _const_spec((1, LANES)),
            _const_spec((2 * LANES, D_INNER)),
        ]

    scratch = [pltpu.VMEM((GROUPS, NSTATE, GROUP_COLS), F32)]
    params = pltpu.CompilerParams(
        dimension_semantics=("arbitrary", "arbitrary"), vmem_limit_bytes=VMEM_LIMIT)

    y_fw = pl.pallas_call(
        functools.partial(_ssd_fwd_kernel, nchunks=nchunks),
        grid=(bsz, nb),
        in_specs=specs(lambda j: j) + [_const_spec((1, D_INNER))],
        out_specs=pl.BlockSpec((1, tb, D_INNER), lambda b, j: (b, j, 0)),
        out_shape=jax.ShapeDtypeStruct((bsz, seq, D_INNER), BF16),
        scratch_shapes=scratch,
        compiler_params=params,
        name="ssd_forward",
    )(xc, dt, alog_row, _expansion(0), dskip_row)

    bwd_pos = lambda j: nb - 1 - j
    tok = lambda width: pl.BlockSpec((1, tb, width), lambda b, j: (b, bwd_pos(j), 0))
    return pl.pallas_call(
        functools.partial(_ssd_bwd_kernel, nchunks=nchunks),
        grid=(bsz, nb),
        in_specs=specs(bwd_pos) + [tok(D_INNER), tok(D_INNER), _const_spec((1, D_INNER))],
        out_specs=tok(D_INNER),
        out_shape=jax.ShapeDtypeStruct((bsz, seq, D_INNER), BF16),
        scratch_shapes=scratch,
        compiler_params=params,
        name="ssd_backward",
    )(xc, dt, alog_row, _expansion(HEADS), y_fw, zs, norm_w.reshape(1, D_INNER))


def _merge_kernel(h_ref, mod_ref, yn_ref, vm_ref, vp_ref, vn_ref, gates_ref, cw_ref, cb_ref, lng_ref, lnb_ref,
                  wpa_ref, wpb_ref, wo_ref, o_ref, sh_ref, vc_ref, *, tm):
    i = pl.program_id(1)
    ni = pl.num_programs(1)
    ext = tm + 2 * HALO
    zero = jnp.zeros((HALO, D_MODEL), F32)
    sh_ref[0, 0:HALO, :] = jnp.where(i == 0, zero, vp_ref[0].astype(F32))
    sh_ref[0, HALO:HALO + tm, :] = vm_ref[0].astype(F32)
    sh_ref[0, HALO + tm:ext, :] = jnp.where(i == ni - 1, zero, vn_ref[0].astype(F32))
    for c in range(D_MODEL // CONV_COLS):
        cols = slice(c * CONV_COLS, (c + 1) * CONV_COLS)
        n_tiles = ext // SUBLANES
        tiles = [sh_ref[0, t * SUBLANES:(t + 1) * SUBLANES, cols] for t in range(n_tiles)]
        shifted = _row_windows(tiles, n_tiles - 1, list(range(1, SUBLANES)))
        for r in range(1, SUBLANES):
            sh_ref[r, 0:ext - SUBLANES, cols] = shifted[r]
        for rb in range(tm // CONV_ROWS):
            acc = jnp.broadcast_to(cb_ref[:, cols], (CONV_ROWS, CONV_COLS))
            for k in range(CONF_K):
                off = HALO - CONF_K // 2 + k
                start = rb * CONV_ROWS + (off // SUBLANES) * SUBLANES
                acc = acc + cw_ref[k:k + 1, cols] * sh_ref[off % SUBLANES, start:start + CONV_ROWS, cols]
            vc_ref[rb * CONV_ROWS:(rb + 1) * CONV_ROWS, cols] = acc
    vc = vc_ref[...]
    mu = jnp.mean(vc, axis=-1, keepdims=True)
    cen = vc - mu
    var = jnp.mean(cen * cen, axis=-1, keepdims=True)
    vb = _silu(cen * lax.rsqrt(var + EPS) * lng_ref[...] + lnb_ref[...]).astype(BF16)
    br_b = _dot(vb, wpb_ref[0])
    br_a = _dot(yn_ref[0], wpa_ref[0])
    g_a = gates_ref[0, :, 0:D_MODEL].astype(F32)
    g_b = gates_ref[0, :, D_MODEL:2 * D_MODEL].astype(F32)
    m = (g_a * br_a + g_b * br_b).astype(BF16)
    o_ref[0] = h_ref[0] + mod_ref[0, 5:6, :] * _dot(m, wo_ref[0])


def _merge(h, mods, yn, v, gates, conv_w, conv_b, ln_g, ln_b, w_pa, w_pb, w_out, *, layer, tm=256):
    bsz, seq, _ = h.shape
    hb = tm // HALO
    nhb = seq // HALO
    cw = jnp.concatenate([conv_w, jnp.zeros((32 - CONF_K, D_MODEL), F32)], axis=0)

    def tok(width):
        return pl.BlockSpec((1, tm, width), lambda b, i: (b, i, 0))

    row = lambda a: a.reshape(1, D_MODEL)
    return pl.pallas_call(
        functools.partial(_merge_kernel, tm=tm),
        grid=(bsz, seq // tm),
        in_specs=[
            tok(D_MODEL),
            pl.BlockSpec((1, N_MOD, D_MODEL), lambda b, i: (b, 0, 0)),
            tok(D_INNER),
            tok(D_MODEL),
            pl.BlockSpec((1, HALO, D_MODEL), lambda b, i: (b, jnp.maximum(i * hb - 1, 0), 0)),
            pl.BlockSpec((1, HALO, D_MODEL), lambda b, i: (b, jnp.minimum((i + 1) * hb, nhb - 1), 0)),
            tok(2 * D_MODEL),
            _const_spec((32, D_MODEL)),
            _const_spec((1, D_MODEL)),
            _const_spec((1, D_MODEL)),
            _const_spec((1, D_MODEL)),
            _layer_spec((D_INNER, D_MODEL), layer),
            _layer_spec((D_MODEL, D_MODEL), layer),
            _layer_spec((D_MODEL, D_MODEL), layer),
        ],
        out_specs=tok(D_MODEL),
        out_shape=jax.ShapeDtypeStruct(h.shape, F32),
        scratch_shapes=[pltpu.VMEM((SUBLANES, tm + 2 * HALO, D_MODEL), F32), pltpu.VMEM((tm, D_MODEL), F32)],
        compiler_params=pltpu.CompilerParams(
            dimension_semantics=("parallel", "parallel"), vmem_limit_bytes=VMEM_LIMIT),
        name="mixer_merge",
    )(h, mods, yn, v, v, v, gates, cw, row(conv_b), row(ln_g), row(ln_b), w_pa, w_pb, w_out)


def _trunk(x, mods_all, w):
    depth = w["w_in"].shape[0]
    h = x
    for l in range(depth):
        mods = mods_all[l]
        h = _ffn(h, mods, w["ffn1_norm"][l], w["ffn1_up"], w["ffn1_down"], w["final_norm"],
                 layer=l, mod_base=0, final=False)
        zs, xc, v, gates, dt = _inproj(h, mods, w["mix_norm"][l], w["w_in"], w["dt_bias"][l],
                                       w["ssm_conv_w"][l], w["ssm_conv_b"][l], layer=l)
        yn = _ssd(xc, dt, zs, w["a_log"][l], w["d_skip"][l], w["ssm_norm"][l])
        h = _merge(h, mods, yn, v, gates, w["conf_conv_w"][l], w["conf_conv_b"][l], w["conf_ln_g"][l],
                   w["conf_ln_b"][l], w["w_proj_ssd"], w["w_proj_conv"], w["w_out"], layer=l)
        h = _ffn(h, mods, w["ffn2_norm"][l], w["ffn2_up"], w["ffn2_down"], w["final_norm"],
                 layer=l, mod_base=6, final=(l == depth - 1))
    return h


def kernel(x_prompt, x_sample, c_prompt, c_sample, w_ada, b_ada, ffn1_norm, ffn1_up, ffn1_down, mix_norm, w_in,
           ssm_conv_w, ssm_conv_b, dt_bias, a_log, d_skip, ssm_norm, w_proj_ssd, conf_conv_w, conf_conv_b,
           conf_ln_g, conf_ln_b, w_proj_conv, w_out, ffn2_norm, ffn2_up, ffn2_down, final_norm):
    bf = lambda a: a.astype(BF16)
    w = dict(ffn1_norm=ffn1_norm, ffn1_up=bf(ffn1_up), ffn1_down=bf(ffn1_down), mix_norm=mix_norm,
             w_in=_inproj_weight(w_in), ssm_conv_w=ssm_conv_w, ssm_conv_b=ssm_conv_b, dt_bias=dt_bias,
             a_log=a_log, d_skip=d_skip, ssm_norm=ssm_norm, w_proj_ssd=bf(w_proj_ssd), conf_conv_w=conf_conv_w,
             conf_conv_b=conf_conv_b, conf_ln_g=conf_ln_g, conf_ln_b=conf_ln_b, w_proj_conv=bf(w_proj_conv),
             w_out=bf(w_out), ffn2_norm=ffn2_norm, ffn2_up=bf(ffn2_up), ffn2_down=bf(ffn2_down),
             final_norm=final_norm)
    nb = x_prompt.shape[0]
    depth = w_ada.shape[0]
    c_all = jnp.concatenate([c_prompt, c_sample], axis=0)
    mods = _mods(c_all, w_ada, b_ada).reshape(depth, c_all.shape[0], N_MOD, D_MODEL)
    y_prompt = _trunk(x_prompt, mods[:, :nb], w)
    y_sample = _trunk(x_sample, mods[:, nb:], w)
    return (y_prompt, y_sample)
```

```python
import functools
import math

import jax
import jax.numpy as jnp
from jax import lax
from jax.experimental import pallas as pl
from jax.experimental.pallas import tpu as pltpu

D_MODEL = 1024
D_INNER = 2 * D_MODEL
HEADDIM = 64
HEADS = D_INNER // HEADDIM
GROUPS = 4
HEADS_PER_GROUP = HEADS // GROUPS
GROUP_COLS = HEADS_PER_GROUP * HEADDIM
NSTATE = 128
CHUNK = 128
SSM_K = 5
CONV_DIM = D_INNER + 2 * GROUPS * NSTATE
CONF_K = 31
FFN_DIM = 2816
N_MOD = 9
EPS = 1e-6
OFF_XBC = D_INNER
OFF_DT = OFF_XBC + CONV_DIM
OFF_GLU = OFF_DT + 2 * HEADS
OFF_GATE = OFF_GLU + 2 * D_MODEL
IN_COLS = OFF_GATE + 2 * D_MODEL

LANES = 128
SUBLANES = 8
HALO = 16
FFN_COLS = 256
PROJ_COLS = 512
CHUNK_UNROLL = 4
CONV_ROWS = 64
CONV_COLS = 2 * LANES
VMEM_LIMIT = 56 * 1024 * 1024
LOG2E = math.log2(math.e)

F32 = jnp.float32
BF16 = jnp.bfloat16


def _dot(a, b):
    return jnp.dot(a, b, preferred_element_type=F32)


def _rms_mod(x, norm_w, shift, scale):
    y = x * lax.rsqrt(jnp.mean(x * x, axis=-1, keepdims=True) + EPS)
    return (y * norm_w) * (1.0 + scale) + shift


def _silu(x):
    return x * jax.nn.sigmoid(x)


def _row_windows(tiles, n_out, offsets):
    width = tiles[0].shape[1]
    sub = lax.broadcasted_iota(jnp.int32, (SUBLANES, width), 0)
    rolled = {}
    out = {}
    for o in offsets:
        q, r = divmod(o, SUBLANES)
        if r == 0:
            out[o] = jnp.concatenate(tiles[q:q + n_out], axis=0)
            continue
        if r not in rolled:
            rolled[r] = [pltpu.roll(t, SUBLANES - r, axis=0) for t in tiles]
        from_next = sub >= SUBLANES - r
        rt = rolled[r]
        out[o] = jnp.concatenate(
            [jnp.where(from_next, rt[q + i + 1], rt[q + i]) for i in range(n_out)], axis=0)
    return out


def _const_spec(shape):
    nd = len(shape)
    return pl.BlockSpec(shape, lambda *_: (0,) * nd, pipeline_mode=pl.Buffered(1))


def _layer_spec(shape, layer):
    nd = len(shape)
    return pl.BlockSpec((1,) + tuple(shape), lambda *_: (layer,) + (0,) * nd, pipeline_mode=pl.Buffered(1))


def _mods_kernel(c_ref, w_ref, b_ref, o_ref):
    c_act = _silu(c_ref[...])
    o_ref[0] = jnp.dot(c_act, w_ref[0], preferred_element_type=F32,
                       precision=lax.Precision.HIGHEST) + b_ref[0]


def _mods(c_all, w_ada, b_ada):
    depth, _, cols = w_ada.shape
    rows = c_all.shape[0]
    tn = D_MODEL
    return pl.pallas_call(
        _mods_kernel,
        grid=(depth, cols // tn),
        in_specs=[
            pl.BlockSpec((rows, D_MODEL), lambda l, j: (0, 0)),
            pl.BlockSpec((1, D_MODEL, tn), lambda l, j: (l, 0, j)),
            pl.BlockSpec((1, 1, tn), lambda l, j: (l, 0, j)),
        ],
        out_specs=pl.BlockSpec((1, rows, tn), lambda l, j: (l, 0, j)),
        out_shape=jax.ShapeDtypeStruct((depth, rows, cols), F32),
        name="adaln_mods",
    )(c_all, w_ada, b_ada.reshape(depth, 1, cols))


def _ffn_kernel(h_ref, mod_ref, nw_ref, wup_ref, wdn_ref, fnw_ref, o_ref, act_ref, *, mod_base, final):
    x = h_ref[0]
    shift = mod_ref[0, mod_base:mod_base + 1, :]
    scale = mod_ref[0, mod_base + 1:mod_base + 2, :]
    gate = mod_ref[0, mod_base + 2:mod_base + 3, :]
    u = _rms_mod(x, nw_ref[...], shift, scale).astype(BF16)
    for j in range(FFN_DIM // FFN_COLS):
        cols = slice(j * FFN_COLS, (j + 1) * FFN_COLS)
        gcols = slice(FFN_DIM + j * FFN_COLS, FFN_DIM + (j + 1) * FFN_COLS)
        a = _dot(u, wup_ref[0, :, cols])
        b = _dot(u, wup_ref[0, :, gcols])
        act_ref[:, cols] = (_silu(a) * b).astype(BF16)
    y = _dot(act_ref[...], wdn_ref[0])
    out = x + (0.5 * gate) * y
    if final:
        out = out * lax.rsqrt(jnp.mean(out * out, axis=-1, keepdims=True) + EPS) * fnw_ref[...]
    o_ref[0] = out


def _ffn(h, mods, norm_w, w_up, w_down, final_w, *, layer, mod_base, final, tm=512):
    bsz, seq, _ = h.shape
    kern = functools.partial(_ffn_kernel, mod_base=mod_base, final=final)
    return pl.pallas_call(
        kern,
        grid=(bsz, seq // tm),
        in_specs=[
            pl.BlockSpec((1, tm, D_MODEL), lambda b, i: (b, i, 0)),
            pl.BlockSpec((1, N_MOD, D_MODEL), lambda b, i: (b, 0, 0)),
            _const_spec((1, D_MODEL)),
            _layer_spec((D_MODEL, 2 * FFN_DIM), layer),
            _layer_spec((FFN_DIM, D_MODEL), layer),
            _const_spec((1, D_MODEL)),
        ],
        out_specs=pl.BlockSpec((1, tm, D_MODEL), lambda b, i: (b, i, 0)),
        out_shape=jax.ShapeDtypeStruct(h.shape, F32),
        scratch_shapes=[pltpu.VMEM((tm, FFN_DIM), BF16)],
        compiler_params=pltpu.CompilerParams(
            dimension_semantics=("parallel", "parallel"), vmem_limit_bytes=VMEM_LIMIT),
        name="ffn_final" if final else "ffn",
    )(h, mods, norm_w.reshape(1, D_MODEL), w_up, w_down, final_w.reshape(1, D_MODEL))


PW_Z = 0
PW_XBC = PW_Z + D_INNER
PW_GLU = PW_XBC + CONV_DIM
PW_GATE = PW_GLU + 2 * D_MODEL
PW_DT = PW_GATE + 2 * D_MODEL
PW_COLS = PW_DT + LANES


def _inproj_kernel(h_ref, hp_ref, hn_ref, mod_ref, nw_ref, w_ref, dtb_ref, cw_ref, cb_ref,
                   zs_ref, xc_ref, v_ref, gates_ref, dt_ref, u_ref, p_ref, *, tm):
    i = pl.program_id(1)
    first = i == 0
    last = i == pl.num_programs(1) - 1
    hx = jnp.concatenate([hp_ref[0], h_ref[0], hn_ref[0]], axis=0)
    u_ref[...] = _rms_mod(hx, nw_ref[...], mod_ref[0, 3:4, :], mod_ref[0, 4:5, :]).astype(BF16)
    zero = jnp.zeros((HALO, PROJ_COLS), F32)
    u_ext = slice(0, tm + 2 * HALO)
    u = slice(HALO, HALO + tm)

    def proj(rows, base, j):
        return _dot(u_ref[rows, :], w_ref[0, :, base + j * PROJ_COLS:base + (j + 1) * PROJ_COLS])

    def z_unit(j):
        zs_ref[0, :, j * PROJ_COLS:(j + 1) * PROJ_COLS] = _silu(proj(u, PW_Z, j)).astype(BF16)

    def conv_unit(j):
        p = proj(u_ext, PW_XBC, j)
        p_ref[0:HALO, :] = jnp.where(first, zero, p[0:HALO])
        p_ref[HALO:HALO + tm, :] = p[HALO:HALO + tm]
        p_ref[HALO + tm:, :] = jnp.where(last, zero, p[HALO + tm:])
        for c in range(PROJ_COLS // CONV_COLS):
            pc = slice(c * CONV_COLS, (c + 1) * CONV_COLS)
            cols = slice(j * PROJ_COLS + c * CONV_COLS, j * PROJ_COLS + (c + 1) * CONV_COLS)
            for rb in range(tm // CONV_ROWS):
                acc = jnp.broadcast_to(cb_ref[:, cols], (CONV_ROWS, CONV_COLS))
                for k in range(SSM_K):
                    start = HALO - SSM_K // 2 + k + rb * CONV_ROWS
                    acc = acc + cw_ref[k:k + 1, cols] * p_ref[start:start + CONV_ROWS, pc]
                xc_ref[0, rb * CONV_ROWS:(rb + 1) * CONV_ROWS, cols] = _silu(acc).astype(BF16)

    def glu_unit(j):
        va = proj(u, PW_GLU, j)
        vg = proj(u, PW_GLU + D_MODEL, j)
        v_ref[0, :, j * PROJ_COLS:(j + 1) * PROJ_COLS] = (va * jax.nn.sigmoid(vg)).astype(BF16)

    def gate_unit(j):
        gates_ref[0, :, j * PROJ_COLS:(j + 1) * PROJ_COLS] = jax.nn.sigmoid(proj(u, PW_GATE, j)).astype(BF16)

    light = ([functools.partial(z_unit, j) for j in range(D_INNER // PROJ_COLS)]
             + [functools.partial(glu_unit, j) for j in range(D_MODEL // PROJ_COLS)]
             + [functools.partial(gate_unit, j) for j in range(2 * D_MODEL // PROJ_COLS)])
    n_conv = CONV_DIM // PROJ_COLS
    for j in range(n_conv):
        conv_unit(j)
        for unit in light[j * len(light) // n_conv:(j + 1) * len(light) // n_conv]:
            unit()
    dtr = _dot(u_ref[HALO:HALO + tm, :], w_ref[0, :, PW_DT:PW_DT + LANES]) + dtb_ref[...]
    dt_ref[0] = jnp.maximum(dtr, 0.0) + jnp.log(1.0 + jnp.exp(-jnp.abs(dtr)))


def _inproj_weight(w_in):
    wb = w_in.astype(BF16)
    pad = jnp.zeros(wb.shape[:2] + (LANES - 2 * HEADS,), BF16)
    return jnp.concatenate([wb[..., :OFF_DT], wb[..., OFF_GLU:], wb[..., OFF_DT:OFF_GLU], pad], axis=-1)


def _inproj(h, mods, norm_w, w, dt_bias, conv_w, conv_b, *, layer, tm=512):
    bsz, seq, _ = h.shape
    hb = tm // HALO
    nhb = seq // HALO
    dtb = jnp.concatenate([dt_bias.reshape(1, 2 * HEADS), jnp.zeros((1, LANES - 2 * HEADS), F32)], axis=1)
    cw = jnp.concatenate([conv_w, jnp.zeros((SUBLANES - SSM_K, CONV_DIM), F32)], axis=0)

    def tok(width):
        return pl.BlockSpec((1, tm, width), lambda b, i: (b, i, 0))

    return pl.pallas_call(
        functools.partial(_inproj_kernel, tm=tm),
        grid=(bsz, seq // tm),
        in_specs=[
            tok(D_MODEL),
            pl.BlockSpec((1, HALO, D_MODEL), lambda b, i: (b, jnp.maximum(i * hb - 1, 0), 0)),
            pl.BlockSpec((1, HALO, D_MODEL), lambda b, i: (b, jnp.minimum((i + 1) * hb, nhb - 1), 0)),
            pl.BlockSpec((1, N_MOD, D_MODEL), lambda b, i: (b, 0, 0)),
            _const_spec((1, D_MODEL)),
            _layer_spec((D_MODEL, PW_COLS), layer),
            _const_spec((1, LANES)),
            _const_spec((SUBLANES, CONV_DIM)),
            _const_spec((1, CONV_DIM)),
        ],
        out_specs=[tok(D_INNER), tok(CONV_DIM), tok(D_MODEL), tok(2 * D_MODEL), tok(LANES)],
        out_shape=[
            jax.ShapeDtypeStruct((bsz, seq, D_INNER), BF16),
            jax.ShapeDtypeStruct((bsz, seq, CONV_DIM), BF16),
            jax.ShapeDtypeStruct((bsz, seq, D_MODEL), BF16),
            jax.ShapeDtypeStruct((bsz, seq, 2 * D_MODEL), BF16),
            jax.ShapeDtypeStruct((bsz, seq, LANES), F32),
        ],
        scratch_shapes=[pltpu.VMEM((tm + 2 * HALO, D_MODEL), BF16),
                        pltpu.VMEM((tm + 2 * HALO, PROJ_COLS), F32)],
        compiler_params=pltpu.CompilerParams(
            dimension_semantics=("parallel", "parallel"), vmem_limit_bytes=VMEM_LIMIT),
        name="mixer_inproj",
    )(h, h, h, mods, norm_w.reshape(1, D_MODEL), w, dtb, cw, conv_b.reshape(1, CONV_DIM))


def _split2(v):
    hi = v.astype(BF16)
    lo = (v - hi.astype(F32)).astype(BF16)
    return jnp.concatenate([hi, lo], axis=1)


def _tri_cumsum(tri, v):
    p1 = v.astype(BF16)
    r1 = v - p1.astype(F32)
    p2 = r1.astype(BF16)
    p3 = (r1 - p2.astype(F32)).astype(BF16)
    return _dot(tri, p1) + _dot(tri, p2) + _dot(tri, p3)


def _ssd_chunk(xc_ref, dt_ref, e2_ref, state_ref, rows, a_row, mask, tri, left, *, reverse, head_off):
    dt = dt_ref[0, rows, :]
    cs2 = _tri_cumsum(tri, dt * a_row) * LOG2E
    tot2 = cs2[0:1, :] if reverse else cs2[CHUNK - 1:CHUNK, :]
    src_t = (jnp.log(dt) * LOG2E - cs2).T
    ecs = jnp.exp2(cs2)
    dtw = dt * jnp.exp2(tot2 - cs2)
    etot = jnp.broadcast_to(jnp.exp2(tot2), (SUBLANES, LANES))
    wide = _dot(jnp.concatenate([_split2(ecs), _split2(dtw), _split2(etot)], axis=0), e2_ref[...])
    ecs_w = wide[0:CHUNK]
    dtw_w = wide[CHUNK:2 * CHUNK].astype(BF16)
    etot_w = wide[2 * CHUNK:2 * CHUNK + 1]

    y_groups = []
    for g in range(GROUPS):
        gcols = slice(g * GROUP_COLS, (g + 1) * GROUP_COLS)
        b_g = xc_ref[0, rows, D_INNER + g * NSTATE:D_INNER + (g + 1) * NSTATE]
        c_off = D_INNER + GROUPS * NSTATE + g * NSTATE
        c_g = xc_ref[0, rows, c_off:c_off + NSTATE]
        cb = lax.dot_general(c_g, b_g, (((1,), (1,)), ((), ())), preferred_element_type=F32)
        s_old = state_ref[g]
        y_off = _dot(c_g, s_old.astype(BF16)) * ecs_w[:, gcols]
        y_pairs = []
        for p in range(HEADS_PER_GROUP // 2):
            h0 = g * HEADS_PER_GROUP + 2 * p
            x_pair = xc_ref[0, rows, h0 * HEADDIM:(h0 + 2) * HEADDIM]
            mats = []
            for cc in (head_off + h0, head_off + h0 + 1):
                arg = cs2[:, cc:cc + 1] + src_t[cc:cc + 1, :]
                mats.append((cb * jnp.exp2(jnp.where(mask, arg, -jnp.inf))).astype(BF16))
            zero = jnp.zeros_like(x_pair)
            rhs = jnp.concatenate([jnp.where(left, x_pair, zero), jnp.where(left, zero, x_pair)], axis=0)
            y_pairs.append(_dot(jnp.concatenate(mats, axis=1), rhs))
        y_groups.append(jnp.concatenate(y_pairs, axis=1) + y_off)
        xw = xc_ref[0, rows, gcols] * dtw_w[:, gcols]
        s_new = lax.dot_general(b_g, xw, (((0,), (0,)), ((), ())), preferred_element_type=F32)
        state_ref[g] = s_old * etot_w[:, gcols] + s_new
    return jnp.concatenate(y_groups, axis=1)


def _ssd_setup(alog_ref, state_ref, *, reverse):
    @pl.when(pl.program_id(1) == 0)
    def _():
        state_ref[...] = jnp.zeros_like(state_ref)

    row = lax.broadcasted_iota(jnp.int32, (CHUNK, CHUNK), 0)
    col = lax.broadcasted_iota(jnp.int32, (CHUNK, CHUNK), 1)
    mask = (col >= row) if reverse else (row >= col)
    left = lax.broadcasted_iota(jnp.int32, (CHUNK, LANES), 1) < HEADDIM
    return -jnp.exp(alog_ref[...]), mask, mask.astype(BF16), left


def _ssd_fwd_kernel(xc_ref, dt_ref, alog_ref, e2_ref, dskip_ref, y_ref, state_ref, *, nchunks):
    a_row, mask, tri, left = _ssd_setup(alog_ref, state_ref, reverse=False)

    def body(c, carry):
        rows = pl.ds(pl.multiple_of(c * CHUNK, CHUNK), CHUNK)
        y = _ssd_chunk(xc_ref, dt_ref, e2_ref, state_ref, rows, a_row, mask, tri, left,
                       reverse=False, head_off=0)
        y = y + dskip_ref[...] * xc_ref[0, rows, 0:D_INNER].astype(F32)
        y_ref[0, rows, :] = y.astype(y_ref.dtype)
        return carry

    lax.fori_loop(0, nchunks, body, 0, unroll=CHUNK_UNROLL)


def _ssd_bwd_kernel(xc_ref, dt_ref, alog_ref, e2_ref, yf_ref, zs_ref, nw_ref, o_ref, state_ref, *, nchunks):
    a_row, mask, tri, left = _ssd_setup(alog_ref, state_ref, reverse=True)

    def body(i, carry):
        rows = pl.ds(pl.multiple_of((nchunks - 1 - i) * CHUNK, CHUNK), CHUNK)
        y = _ssd_chunk(xc_ref, dt_ref, e2_ref, state_ref, rows, a_row, mask, tri, left,
                       reverse=True, head_off=HEADS)
        y = y + yf_ref[0, rows, :].astype(F32)
        yg = y * zs_ref[0, rows, :].astype(F32)
        yn = yg * lax.rsqrt(jnp.mean(yg * yg, axis=-1, keepdims=True) + EPS) * nw_ref[...]
        o_ref[0, rows, :] = yn.astype(o_ref.dtype)
        return carry

    lax.fori_loop(0, nchunks, body, 0, unroll=CHUNK_UNROLL)


def _expansion(head_off):
    r = lax.broadcasted_iota(jnp.int32, (LANES, D_INNER), 0)
    c = lax.broadcasted_iota(jnp.int32, (LANES, D_INNER), 1)
    e = (r == head_off + c // HEADDIM).astype(BF16)
    return jnp.concatenate([e, e], axis=0)


def _ssd(xc, dt, zs, a_log, d_skip, norm_w, *, tb=1024):
    bsz, seq, _ = xc.shape
    nb = seq // tb
    nchunks = tb // CHUNK
    pad = jnp.zeros((1, LANES - 2 * HEADS), F32)
    alog_row = jnp.concatenate([a_log.reshape(1, 2 * HEADS), pad], axis=1)
    dskip_row = jnp.repeat(d_skip, HEADDIM).reshape(1, D_INNER)

    def specs(pos):
        return [
            pl.BlockSpec((1, tb, CONV_DIM), lambda b, j: (b, pos(j), 0)),
            pl.BlockSpec((1, tb, LANES), lambda b, j: (b, pos(j), 0)),
            _const_spec((1, LANES)),
            _const_spec((2 * LANES, D_INNER)),
        ]

    scratch = [pltpu.VMEM((GROUPS, NSTATE, GROUP_COLS), F32)]
    params = pltpu.CompilerParams(
        dimension_semantics=("arbitrary", "arbitrary"), vmem_limit_bytes=VMEM_LIMIT)

    y_fw = pl.pallas_call(
        functools.partial(_ssd_fwd_kernel, nchunks=nchunks),
        grid=(bsz, nb),
        in_specs=specs(lambda j: j) + [_const_spec((1, D_INNER))],
        out_specs=pl.BlockSpec((1, tb, D_INNER), lambda b, j: (b, j, 0)),
        out_shape=jax.ShapeDtypeStruct((bsz, seq, D_INNER), BF16),
        scratch_shapes=scratch,
        compiler_params=params,
        name="ssd_forward",
    )(xc, dt, alog_row, _expansion(0), dskip_row)

    bwd_pos = lambda j: nb - 1 - j
    tok = lambda width: pl.BlockSpec((1, tb, width), lambda b, j: (b, bwd_pos(j), 0))
    return pl.pallas_call(
        functools.partial(_ssd_bwd_kernel, nchunks=nchunks),
        grid=(bsz, nb),
        in_specs=specs(bwd_pos) + [tok(D_INNER), tok(D_INNER), _const_spec((1, D_INNER))],
        out_specs=tok(D_INNER),
        out_shape=jax.ShapeDtypeStruct((bsz, seq, D_INNER), BF16),
        scratch_shapes=scratch,
        compiler_params=params,
        name="ssd_backward",
    )(xc, dt, alog_row, _expansion(HEADS), y_fw, zs, norm_w.reshape(1, D_INNER))


def _merge_kernel(h_ref, mod_ref, yn_ref, vm_ref, vp_ref, vn_ref, gates_ref, cw_ref, cb_ref, lng_ref, lnb_ref,
                  wpa_ref, wpb_ref, wo_ref, o_ref, sh_ref, vc_ref, *, tm):
    i = pl.program_id(1)
    ni = pl.num_programs(1)
    ext = tm + 2 * HALO
    zero = jnp.zeros((HALO, D_MODEL), F32)
    sh_ref[0, 0:HALO, :] = jnp.where(i == 0, zero, vp_ref[0].astype(F32))
    sh_ref[0, HALO:HALO + tm, :] = vm_ref[0].astype(F32)
    sh_ref[0, HALO + tm:ext, :] = jnp.where(i == ni - 1, zero, vn_ref[0].astype(F32))
    for c in range(D_MODEL // CONV_COLS):
        cols = slice(c * CONV_COLS, (c + 1) * CONV_COLS)
        n_tiles = ext // SUBLANES
        tiles = [sh_ref[0, t * SUBLANES:(t + 1) * SUBLANES, cols] for t in range(n_tiles)]
        shifted = _row_windows(tiles, n_tiles - 1, list(range(1, SUBLANES)))
        for r in range(1, SUBLANES):
            sh_ref[r, 0:ext - SUBLANES, cols] = shifted[r]
        for rb in range(tm // CONV_ROWS):
            acc = jnp.broadcast_to(cb_ref[:, cols], (CONV_ROWS, CONV_COLS))
            for k in range(CONF_K):
                off = HALO - CONF_K // 2 + k
                start = rb * CONV_ROWS + (off // SUBLANES) * SUBLANES
                acc = acc + cw_ref[k:k + 1, cols] * sh_ref[off % SUBLANES, start:start + CONV_ROWS, cols]
            vc_ref[rb * CONV_ROWS:(rb + 1) * CONV_ROWS, cols] = acc
    vc = vc_ref[...]
    mu = jnp.mean(vc, axis=-1, keepdims=True)
    cen = vc - mu
    var = jnp.mean(cen * cen, axis=-1, keepdims=True)
    vb = _silu(cen * lax.rsqrt(var + EPS) * lng_ref[...] + lnb_ref[...]).astype(BF16)
    br_b = _dot(vb, wpb_ref[0])
    br_a = _dot(yn_ref[0], wpa_ref[0])
    g_a = gates_ref[0, :, 0:D_MODEL].astype(F32)
    g_b = gates_ref[0, :, D_MODEL:2 * D_MODEL].astype(F32)
    m = (g_a * br_a + g_b * br_b).astype(BF16)
    o_ref[0] = h_ref[0] + mod_ref[0, 5:6, :] * _dot(m, wo_ref[0])


def _merge(h, mods, yn, v, gates, conv_w, conv_b, ln_g, ln_b, w_pa, w_pb, w_out, *, layer, tm=512):
    bsz, seq, _ = h.shape
    hb = tm // HALO
    nhb = seq // HALO
    cw = jnp.concatenate([conv_w, jnp.zeros((32 - CONF_K, D_MODEL), F32)], axis=0)

    def tok(width):
        return pl.BlockSpec((1, tm, width), lambda b, i: (b, i, 0))

    row = lambda a: a.reshape(1, D_MODEL)
    return pl.pallas_call(
        functools.partial(_merge_kernel, tm=tm),
        grid=(bsz, seq // tm),
        in_specs=[
            tok(D_MODEL),
            pl.BlockSpec((1, N_MOD, D_MODEL), lambda b, i: (b, 0, 0)),
            tok(D_INNER),
            tok(D_MODEL),
            pl.BlockSpec((1, HALO, D_MODEL), lambda b, i: (b, jnp.maximum(i * hb - 1, 0), 0)),
            pl.BlockSpec((1, HALO, D_MODEL), lambda b, i: (b, jnp.minimum((i + 1) * hb, nhb - 1), 0)),
            tok(2 * D_MODEL),
            _const_spec((32, D_MODEL)),
            _const_spec((1, D_MODEL)),
            _const_spec((1, D_MODEL)),
            _const_spec((1, D_MODEL)),
            _layer_spec((D_INNER, D_MODEL), layer),
            _layer_spec((D_MODEL, D_MODEL), layer),
            _layer_spec((D_MODEL, D_MODEL), layer),
        ],
        out_specs=tok(D_MODEL),
        out_shape=jax.ShapeDtypeStruct(h.shape, F32),
        scratch_shapes=[pltpu.VMEM((SUBLANES, tm + 2 * HALO, D_MODEL), F32), pltpu.VMEM((tm, D_MODEL), F32)],
        compiler_params=pltpu.CompilerParams(
            dimension_semantics=("parallel", "parallel"), vmem_limit_bytes=VMEM_LIMIT),
        name="mixer_merge",
    )(h, mods, yn, v, v, v, gates, cw, row(conv_b), row(ln_g), row(ln_b), w_pa, w_pb, w_out)


def _trunk(x, mods_all, w):
    depth = w["w_in"].shape[0]
    h = x
    for l in range(depth):
        mods = mods_all[l]
        h = _ffn(h, mods, w["ffn1_norm"][l], w["ffn1_up"], w["ffn1_down"], w["final_norm"],
                 layer=l, mod_base=0, final=False)
        zs, xc, v, gates, dt = _inproj(h, mods, w["mix_norm"][l], w["w_in"], w["dt_bias"][l],
                                       w["ssm_conv_w"][l], w["ssm_conv_b"][l], layer=l)
        yn = _ssd(xc, dt, zs, w["a_log"][l], w["d_skip"][l], w["ssm_norm"][l])
        h = _merge(h, mods, yn, v, gates, w["conf_conv_w"][l], w["conf_conv_b"][l], w["conf_ln_g"][l],
                   w["conf_ln_b"][l], w["w_proj_ssd"], w["w_proj_conv"], w["w_out"], layer=l)
        h = _ffn(h, mods, w["ffn2_norm"][l], w["ffn2_up"], w["ffn2_down"], w["final_norm"],
                 layer=l, mod_base=6, final=(l == depth - 1))
    return h


def kernel(x_prompt, x_sample, c_prompt, c_sample, w_ada, b_ada, ffn1_norm, ffn1_up, ffn1_down, mix_norm, w_in,
           ssm_conv_w, ssm_conv_b, dt_bias, a_log, d_skip, ssm_norm, w_proj_ssd, conf_conv_w, conf_conv_b,
           conf_ln_g, conf_ln_b, w_proj_conv, w_out, ffn2_norm, ffn2_up, ffn2_down, final_norm):
    bf = lambda a: a.astype(BF16)
    w = dict(ffn1_norm=ffn1_norm, ffn1_up=bf(ffn1_up), ffn1_down=bf(ffn1_down), mix_norm=mix_norm,
             w_in=_inproj_weight(w_in), ssm_conv_w=ssm_conv_w, ssm_conv_b=ssm_conv_b, dt_bias=dt_bias,
             a_log=a_log, d_skip=d_skip, ssm_norm=ssm_norm, w_proj_ssd=bf(w_proj_ssd), conf_conv_w=conf_conv_w,
             conf_conv_b=conf_conv_b, conf_ln_g=conf_ln_g, conf_ln_b=conf_ln_b, w_proj_conv=bf(w_proj_conv),
             w_out=bf(w_out), ffn2_norm=ffn2_norm, ffn2_up=bf(ffn2_up), ffn2_down=bf(ffn2_down),
             final_norm=final_norm)
    nb = x_prompt.shape[0]
    depth = w_ada.shape[0]
    c_all = jnp.concatenate([c_prompt, c_sample], axis=0)
    mods = _mods(c_all, w_ada, b_ada).reshape(depth, c_all.shape[0], N_MOD, D_MODEL)
    y_prompt = _trunk(x_prompt, mods[:, :nb], w)
    y_sample = _trunk(x_sample, mods[:, nb:], w)
    return (y_prompt, y_sample)
```
